```python
import jax, jax.numpy as jnp
from jax import lax
import numpy as np

D_MODEL = 1024
BATCH = 8
SEQ = 4096
DEPTH = 2

HEAD_DIM = 64
MIX_WIDTH = D_MODEL
FOX_HEADS = MIX_WIDTH // (4 * HEAD_DIM)
MOBA_HEADS = MIX_WIDTH // (4 * HEAD_DIM)
RWKV_HEADS = MIX_WIDTH // (2 * HEAD_DIM)
FOX_WIDTH = FOX_HEADS * HEAD_DIM
MOBA_WIDTH = MOBA_HEADS * HEAD_DIM
RWKV_WIDTH = RWKV_HEADS * HEAD_DIM
RWKV_DECAY_LORA = 64
RWKV_ICLR_LORA = 64
RWKV_GATE_LORA = 160
RWKV_GN_EPS = 64e-5
FOX_IN = 3 * FOX_WIDTH + FOX_HEADS
MOBA_IN = 3 * MOBA_WIDTH
RWKV_IN = 3 * RWKV_WIDTH + RWKV_DECAY_LORA + RWKV_ICLR_LORA + RWKV_GATE_LORA
IN_WIDTH = FOX_IN + MOBA_IN + RWKV_IN
FOX_QBLOCK = 128
MOBA_BLOCK = 256
MOBA_TOPK = 3
MOBA_QCHUNK = 32
ROPE_THETA = 500000.0
ROPE_DIM = HEAD_DIM // 4
FFN_DIM = 2816
MEM_LEN = 256
XATTN_HEADS = 4
XATTN_HEAD_DIM = 128
XATTN_WIDTH = XATTN_HEADS * XATTN_HEAD_DIM
NORM_EPS = 1e-6

kernel_name = 'hybrid_fox_rwkv7_moba_macaron'


def rms_norm(t, g):
    tf = t.astype(jnp.float32)
    y = tf * lax.rsqrt(jnp.mean(tf * tf, axis=-1, keepdims=True) + NORM_EPS)
    return (y * g).astype(t.dtype)


def swiglu(h, w_in, w_out):
    gate, up = jnp.split(h @ w_in, 2, axis=-1)
    return (jax.nn.silu(gate) * up) @ w_out


def split_heads(t, n_heads):
    B, S, _ = t.shape
    return t.reshape(B, S, n_heads, HEAD_DIM).transpose(0, 2, 1, 3)


def merge_heads(t):
    B, H, S, d = t.shape
    return t.transpose(0, 2, 1, 3).reshape(B, S, H * d)


def rope_tables(positions):
    inv_freq = ROPE_THETA ** (-jnp.arange(0, ROPE_DIM, 2, dtype=jnp.float32) / ROPE_DIM)
    ang = positions.astype(jnp.float32)[..., None] * inv_freq
    return jnp.cos(ang), jnp.sin(ang)


def apply_partial_rope(t, cos, sin):
    half = ROPE_DIM // 2
    x1, x2, rest = t[..., :half], t[..., half:ROPE_DIM], t[..., ROPE_DIM:]
    c, s = cos[:, None], sin[:, None]
    out = jnp.concatenate([x1 * c - x2 * s, x2 * c + x1 * s, rest.astype(jnp.float32)], axis=-1)
    return out.astype(t.dtype)


def forgetting_attention(q, k, v, log_f):
    B, H, S, d = q.shape
    c = jnp.cumsum(log_f, axis=-1)
    n_qb = S // FOX_QBLOCK
    q_blocks = q.reshape(B, H, n_qb, FOX_QBLOCK, d).transpose(2, 0, 1, 3, 4)
    c_blocks = c.reshape(B, H, n_qb, FOX_QBLOCK).transpose(2, 0, 1, 3)
    k_pos = jnp.arange(S)
    scale = d ** -0.5

    def block(args):
        i, q_i, c_i = args
        s = jnp.einsum('bhqd,bhkd->bhqk', q_i, k).astype(jnp.float32) * scale
        s = s + c_i[..., None] - c[:, :, None, :]
        q_pos = i * FOX_QBLOCK + jnp.arange(FOX_QBLOCK)
        s = jnp.where(k_pos[None, :] <= q_pos[:, None], s, -jnp.inf)
        p = jax.nn.softmax(s, axis=-1).astype(v.dtype)
        return jnp.einsum('bhqk,bhkd->bhqd', p, v)

    out = lax.map(block, (jnp.arange(n_qb), q_blocks, c_blocks))
    return out.transpose(1, 2, 0, 3, 4).reshape(B, H, S, d)


def fox_mixer(z, f_bias, q_gain, k_gain):
    q, k, v, f = jnp.split(z, [FOX_WIDTH, 2 * FOX_WIDTH, 3 * FOX_WIDTH], axis=-1)
    q = rms_norm(split_heads(q, FOX_HEADS), q_gain)
    k = rms_norm(split_heads(k, FOX_HEADS), k_gain)
    v = split_heads(v, FOX_HEADS)
    log_f = jax.nn.log_sigmoid(f.astype(jnp.float32) + f_bias).transpose(0, 2, 1)
    return merge_heads(forgetting_attention(q, k, v, log_f))


def moba_attention(q, k, v):
    B, H, S, d = q.shape
    s_pad = -(-S // MOBA_BLOCK) * MOBA_BLOCK
    pad = ((0, 0), (0, 0), (0, s_pad - S), (0, 0))
    k_pad = jnp.pad(k, pad)
    v_pad = jnp.pad(v, pad)
    n_blk = s_pad // MOBA_BLOCK
    top_k = min(MOBA_TOPK, n_blk)
    k_blk = k_pad.reshape(B, H, n_blk, MOBA_BLOCK, d)
    v_blk = v_pad.reshape(B, H, n_blk, MOBA_BLOCK, d)
    k_mean = jnp.mean(k_blk.astype(jnp.float32), axis=3)
    n_chunk = S // MOBA_QCHUNK
    q_chunks = q.reshape(B, H, n_chunk, MOBA_QCHUNK, d).transpose(2, 0, 1, 3, 4)
    b_idx = jnp.arange(B)[:, None, None, None]
    h_idx = jnp.arange(H)[None, :, None, None]
    blk_ids = jnp.arange(n_blk)
    scale = d ** -0.5

    def chunk(args):
        i, q_i = args
        q_start = i * MOBA_QCHUNK
        q_blk = q_start // MOBA_BLOCK
        gate = jnp.einsum('bhqd,bhnd->bhqn', q_i.astype(jnp.float32), k_mean)
        gate = jnp.where(blk_ids < q_blk, gate, -jnp.inf)
        _, sel = lax.top_k(gate, top_k)
        valid = sel < q_blk
        k_sel = k_blk[b_idx, h_idx, sel]
        v_sel = v_blk[b_idx, h_idx, sel]
        s_sel = jnp.einsum('bhqd,bhqnkd->bhqnk', q_i, k_sel).astype(jnp.float32) * scale
        s_sel = jnp.where(valid[..., None], s_sel, -jnp.inf)
        s_sel = s_sel.reshape(B, H, MOBA_QCHUNK, top_k * MOBA_BLOCK)
        blk_start = q_blk * MOBA_BLOCK
        k_own = lax.dynamic_slice_in_dim(k_pad, blk_start, MOBA_BLOCK, axis=2)
        v_own = lax.dynamic_slice_in_dim(v_pad, blk_start, MOBA_BLOCK, axis=2)
        s_own = jnp.einsum('bhqd,bhkd->bhqk', q_i, k_own).astype(jnp.float32) * scale
        q_pos = q_start + jnp.arange(MOBA_QCHUNK)
        k_pos = blk_start + jnp.arange(MOBA_BLOCK)
        s_own = jnp.where(k_pos[None, :] <= q_pos[:, None], s_own, -jnp.inf)
        p = jax.nn.softmax(jnp.concatenate([s_sel, s_own], axis=-1), axis=-1).astype(v.dtype)
        p_sel = p[..., :top_k * MOBA_BLOCK].reshape(B, H, MOBA_QCHUNK, top_k, MOBA_BLOCK)
        p_own = p[..., top_k * MOBA_BLOCK:]
        return (jnp.einsum('bhqnk,bhqnkd->bhqd', p_sel, v_sel)
                + jnp.einsum('bhqk,bhkd->bhqd', p_own, v_own))

    out = lax.map(chunk, (jnp.arange(n_chunk), q_chunks))
    return out.transpose(1, 2, 0, 3, 4).reshape(B, H, S, d)


def moba_mixer(z, cos, sin, q_gain, k_gain):
    q, k, v = jnp.split(z, [MOBA_WIDTH, 2 * MOBA_WIDTH], axis=-1)
    q = apply_partial_rope(rms_norm(split_heads(q, MOBA_HEADS), q_gain), cos, sin)
    k = apply_partial_rope(rms_norm(split_heads(k, MOBA_HEADS), k_gain), cos, sin)
    v = split_heads(v, MOBA_HEADS)
    return merge_heads(moba_attention(q, k, v))


def rwkv7_mixer(z, mu, w0, w2, a0, a2, g2, k_k, k_a, r_k, ln_w, ln_b):
    B, S, _ = z.shape
    zf = z.astype(jnp.float32)
    z_prev = jnp.pad(zf, ((0, 0), (1, 0), (0, 0)))[:, :S]
    zf = zf + (z_prev - zf) * mu
    W = RWKV_WIDTH
    r, k, v, w_lo, a_lo, g_lo = jnp.split(
        zf, [W, 2 * W, 3 * W, 3 * W + RWKV_DECAY_LORA, 3 * W + RWKV_DECAY_LORA + RWKV_ICLR_LORA], axis=-1)
    w_raw = w0 + jnp.tanh(w_lo) @ w2
    decay = jnp.exp(-jnp.exp(-jax.nn.softplus(-w_raw) - 0.5))
    a = jax.nn.sigmoid(a0 + a_lo @ a2)
    g = jax.nn.sigmoid(g_lo) @ g2
    kk = k * k_k
    k = k * (1.0 + (a - 1.0) * k_a)

    def hd(t):
        return t.reshape(B, S, RWKV_HEADS, HEAD_DIM)

    r, k, v, decay, a, kk = hd(r), hd(k), hd(v), hd(decay), hd(a), hd(kk)
    kk = kk / jnp.maximum(jnp.sqrt(jnp.sum(kk * kk, axis=-1, keepdims=True)), 1e-12)

    def step(state, inp):
        r_t, w_t, k_t, v_t, kk_t, a_t = inp
        sa = jnp.einsum('bhij,bhj->bhi', state, -kk_t)
        state = (state * w_t[:, :, None, :] + sa[..., None] * (kk_t * a_t)[:, :, None, :]
                 + v_t[..., None] * k_t[:, :, None, :])
        return state, jnp.einsum('bhij,bhj->bhi', state, r_t)

    xs = tuple(t.transpose(1, 0, 2, 3) for t in (r, decay, k, v, kk, a))
    state0 = jnp.zeros((B, RWKV_HEADS, HEAD_DIM, HEAD_DIM), jnp.float32)
    _, ys = lax.scan(step, state0, xs)
    y = ys.transpose(1, 0, 2, 3)
    mean = jnp.mean(y, axis=-1, keepdims=True)
    var = jnp.mean(jnp.square(y - mean), axis=-1, keepdims=True)
    y = ((y - mean) * lax.rsqrt(var + RWKV_GN_EPS)).reshape(B, S, W) * ln_w + ln_b
    bonus = (jnp.sum(r * k * r_k, axis=-1, keepdims=True) * v).reshape(B, S, W)
    return ((y + bonus) * g).astype(z.dtype)


def memory_cross_attention(h, m, w_q, w_kv, q_gain, k_gain, w_o):
    B, S, _ = h.shape
    M = m.shape[1]
    q = rms_norm((h @ w_q).reshape(B, S, XATTN_HEADS, XATTN_HEAD_DIM), q_gain)
    kv = (m @ w_kv).reshape(B, M, 2, XATTN_HEADS, XATTN_HEAD_DIM)
    k = rms_norm(kv[:, :, 0], k_gain)
    v = kv[:, :, 1]
    s = jnp.einsum('bqhd,bkhd->bhqk', q, k).astype(jnp.float32) * (XATTN_HEAD_DIM ** -0.5)
    p = jax.nn.softmax(s, axis=-1).astype(v.dtype)
    o = jnp.einsum('bhqk,bkhd->bqhd', p, v).reshape(B, S, XATTN_WIDTH)
    return o @ w_o


def setup_inputs(seed: int = 0) -> dict:
    key = jax.random.key(seed)
    ks = iter(jax.random.split(key, 64))
    L, D = DEPTH, D_MODEL

    def nrm(shape, scale):
        return jax.random.normal(next(ks), shape, jnp.float32) * scale

    def gain(shape):
        return 1.0 + nrm(shape, 0.02)

    x = nrm((BATCH, SEQ, D), 1.0)
    mem = nrm((BATCH, MEM_LEN, D), 1.0)
    offset = jax.random.randint(next(ks), (BATCH, 1), 0, SEQ, dtype=jnp.int32)
    positions = (offset + jnp.arange(SEQ, dtype=jnp.int32)[None, :]).astype(jnp.int32)
    return {
        'x': x, 'mem': mem, 'positions': positions,
        'ffn1_norm': gain((L, D)),
        'ffn1_w_in': nrm((L, D, 2 * FFN_DIM), D ** -0.5),
        'ffn1_w_out': nrm((L, FFN_DIM, D), FFN_DIM ** -0.5),
        'mix_norm': gain((L, D)),
        'mix_w_in': nrm((L, D, IN_WIDTH), D ** -0.5),
        'mix_w_out': nrm((L, MIX_WIDTH, D), MIX_WIDTH ** -0.5),
        'fox_f_bias': 3.0 + nrm((L, FOX_HEADS), 0.5),
        'fox_q_gain': gain((L, HEAD_DIM)),
        'fox_k_gain': gain((L, HEAD_DIM)),
        'moba_q_gain': gain((L, HEAD_DIM)),
        'moba_k_gain': gain((L, HEAD_DIM)),
        'rwkv_mu': jax.random.uniform(next(ks), (L, RWKV_IN), jnp.float32, 0.0, 1.0),
        'rwkv_w0': nrm((L, RWKV_WIDTH), 1.0) - 1.0,
        'rwkv_w2': nrm((L, RWKV_DECAY_LORA, RWKV_WIDTH), RWKV_DECAY_LORA ** -0.5),
        'rwkv_a0': nrm((L, RWKV_WIDTH), 0.5),
        'rwkv_a2': nrm((L, RWKV_ICLR_LORA, RWKV_WIDTH), RWKV_ICLR_LORA ** -0.5),
        'rwkv_g2': nrm((L, RWKV_GATE_LORA, RWKV_WIDTH), RWKV_GATE_LORA ** -0.5),
        'rwkv_k_k': 0.85 + nrm((L, RWKV_WIDTH), 0.05),
        'rwkv_k_a': 1.0 + nrm((L, RWKV_WIDTH), 0.05),
        'rwkv_r_k': nrm((L, RWKV_HEADS, HEAD_DIM), 0.1),
        'rwkv_ln_w': gain((L, RWKV_WIDTH)),
        'rwkv_ln_b': nrm((L, RWKV_WIDTH), 0.02),
        'xattn_norm': gain((L, D)),
        'xattn_mem_norm': gain((L, D)),
        'xattn_w_q': nrm((L, D, XATTN_WIDTH), D ** -0.5),
        'xattn_w_kv': nrm((L, D, 2 * XATTN_WIDTH), D ** -0.5),
        'xattn_q_gain': gain((L, XATTN_HEAD_DIM)),
        'xattn_k_gain': gain((L, XATTN_HEAD_DIM)),
        'xattn_w_out': nrm((L, XATTN_WIDTH, D), XATTN_WIDTH ** -0.5),
        'ffn2_norm': gain((L, D)),
        'ffn2_w_in': nrm((L, D, 2 * FFN_DIM), D ** -0.5),
        'ffn2_w_out': nrm((L, FFN_DIM, D), FFN_DIM ** -0.5),
    }


def reference(x, mem, positions, ffn1_norm, ffn1_w_in, ffn1_w_out, mix_norm, mix_w_in, mix_w_out,
              fox_f_bias, fox_q_gain, fox_k_gain, moba_q_gain, moba_k_gain,
              rwkv_mu, rwkv_w0, rwkv_w2, rwkv_a0, rwkv_a2, rwkv_g2, rwkv_k_k, rwkv_k_a, rwkv_r_k,
              rwkv_ln_w, rwkv_ln_b, xattn_norm, xattn_mem_norm, xattn_w_q, xattn_w_kv,
              xattn_q_gain, xattn_k_gain, xattn_w_out, ffn2_norm, ffn2_w_in, ffn2_w_out):
    cos, sin = rope_tables(positions)
    for l in range(DEPTH):
        x = x + 0.5 * swiglu(rms_norm(x, ffn1_norm[l]), ffn1_w_in[l], ffn1_w_out[l]).astype(x.dtype)
        z = rms_norm(x, mix_norm[l]) @ mix_w_in[l]
        z_fox, z_moba, z_rwkv = jnp.split(z, [FOX_IN, FOX_IN + MOBA_IN], axis=-1)
        y_fox = fox_mixer(z_fox, fox_f_bias[l], fox_q_gain[l], fox_k_gain[l])
        y_rwkv = rwkv7_mixer(z_rwkv, rwkv_mu[l], rwkv_w0[l], rwkv_w2[l], rwkv_a0[l], rwkv_a2[l],
                             rwkv_g2[l], rwkv_k_k[l], rwkv_k_a[l], rwkv_r_k[l], rwkv_ln_w[l], rwkv_ln_b[l])
        y_moba = moba_mixer(z_moba, cos, sin, moba_q_gain[l], moba_k_gain[l])
        y = jnp.concatenate([y_fox, y_rwkv, y_moba], axis=-1) @ mix_w_out[l]
        x = x + y.astype(x.dtype)
        x = x + memory_cross_attention(rms_norm(x, xattn_norm[l]), rms_norm(mem, xattn_mem_norm[l]),
                                       xattn_w_q[l], xattn_w_kv[l], xattn_q_gain[l], xattn_k_gain[l],
                                       xattn_w_out[l]).astype(x.dtype)
        x = x + 0.5 * swiglu(rms_norm(x, ffn2_norm[l]), ffn2_w_in[l], ffn2_w_out[l]).astype(x.dtype)
    return x
```

```python
import functools
import math

import jax
import jax.numpy as jnp
from jax import lax
from jax.experimental import pallas as pl
from jax.experimental.pallas import tpu as pltpu

F32 = jnp.float32
BF16 = jnp.bfloat16
HIGHEST = lax.Precision.HIGHEST

HEAD_DIM = 64
PAIR = 2 * HEAD_DIM
LANES = 128
NORM_EPS = 1e-6
RWKV_GN_EPS = 64e-5
MOBA_BLOCK = 256
MOBA_TOPK = 3
ROPE_THETA = 500000.0
ROPE_DIM = HEAD_DIM // 4
ROPE_HALF = ROPE_DIM // 2
XATTN_HEADS = 4
XATTN_HEAD_DIM = 128
RWKV_CHUNK = 64
MASKED = -1e30
VMEM_LIMIT = 56 * 1024 * 1024


def _params(sem):
    return pltpu.CompilerParams(dimension_semantics=sem, vmem_limit_bytes=VMEM_LIMIT)


def _dot(a, b):
    return jnp.dot(a.astype(BF16), b.astype(BF16), preferred_element_type=F32)


def _dot_nt(a, b):
    return lax.dot_general(a.astype(BF16), b.astype(BF16), (((1,), (1,)), ((), ())),
                           preferred_element_type=F32)


def _dot_tn(a, b):
    return lax.dot_general(a.astype(BF16), b.astype(BF16), (((0,), (0,)), ((), ())),
                           preferred_element_type=F32)


def _resident(shape):
    nd = len(shape)
    return pl.BlockSpec(shape, lambda *_: (0,) * nd, pipeline_mode=pl.Buffered(1))


def _rms(x, gain):
    ms = jnp.mean(x * x, axis=-1, keepdims=True)
    return x * lax.rsqrt(ms + NORM_EPS) * gain


def _lo_mask():
    return lax.broadcasted_iota(jnp.int32, (1, PAIR), 1) < HEAD_DIM


def _halfsum(x, lo):
    s0 = jnp.sum(jnp.where(lo, x, 0.0), axis=-1, keepdims=True)
    s1 = jnp.sum(jnp.where(lo, 0.0, x), axis=-1, keepdims=True)
    return jnp.where(lo, s0, s1)


def _pair_rms(x, gain, lo):
    ms = _halfsum(x * x, lo) * (1.0 / HEAD_DIM)
    return x * lax.rsqrt(ms + NORM_EPS) * gain


def _ffn_kernel(x_ref, g_ref, wg_ref, wu_ref, wo_ref, o_ref, h_ref, acc_ref, *, nf):
    h_ref[...] = _rms(x_ref[...], g_ref[...]).astype(BF16)
    acc_ref[...] = jnp.zeros_like(acc_ref)

    def body(f, carry):
        h = h_ref[...]
        gate = jnp.dot(h, wg_ref[f], preferred_element_type=F32)
        up = jnp.dot(h, wu_ref[f], preferred_element_type=F32)
        act = (gate * jax.nn.sigmoid(gate) * up).astype(BF16)
        acc_ref[...] += jnp.dot(act, wo_ref[f], preferred_element_type=F32)
        return carry

    lax.fori_loop(0, nf, body, 0)
    o_ref[...] = x_ref[...] + 0.5 * acc_ref[...]


def _ffn(x2, gain, w_in, w_out, *, tm, tf):
    n, d = x2.shape
    ffn = w_out.shape[0]
    nf = ffn // tf
    wg = w_in[:, :ffn].astype(BF16).reshape(d, nf, tf).transpose(1, 0, 2)
    wu = w_in[:, ffn:].astype(BF16).reshape(d, nf, tf).transpose(1, 0, 2)
    wo = w_out.astype(BF16).reshape(nf, tf, d)
    return pl.pallas_call(
        functools.partial(_ffn_kernel, nf=nf),
        grid=(n // tm,),
        in_specs=[pl.BlockSpec((tm, d), lambda i: (i, 0)),
                  _resident((1, d)), _resident((nf, d, tf)), _resident((nf, d, tf)),
                  _resident((nf, tf, d))],
        out_specs=pl.BlockSpec((tm, d), lambda i: (i, 0)),
        out_shape=jax.ShapeDtypeStruct((n, d), F32),
        scratch_shapes=[pltpu.VMEM((tm, d), BF16), pltpu.VMEM((tm, d), F32)],
        compiler_params=_params(("parallel",)),
        name="ffn",
    )(x2, gain.reshape(1, d), wg, wu, wo)


def _proj_kernel(x_ref, g_ref, w_ref, o_ref, h_ref):
    @pl.when(pl.program_id(1) == 0)
    def _():
        h_ref[...] = _rms(x_ref[...], g_ref[...]).astype(BF16)

    o_ref[...] = jnp.dot(h_ref[...], w_ref[...], preferred_element_type=F32)


def _proj_t_kernel(x_ref, g_ref, w_ref, wt_ref, o_ref, ot_ref, h_ref):
    @pl.when(pl.program_id(1) == 0)
    def _():
        h_ref[...] = _rms(x_ref[...], g_ref[...]).astype(BF16)
        ot_ref[...] = _dot_nt(wt_ref[...], h_ref[...])

    o_ref[...] = jnp.dot(h_ref[...], w_ref[...], preferred_element_type=F32)


def _norm_proj(x2, gain, w, *, tm, tn, wt=None):
    n, d = x2.shape
    nc = w.shape[1]
    in_specs = [pl.BlockSpec((tm, d), lambda i, j: (i, 0)),
                pl.BlockSpec((1, d), lambda i, j: (0, 0)),
                pl.BlockSpec((d, tn), lambda i, j: (0, j))]
    out_specs = pl.BlockSpec((tm, tn), lambda i, j: (i, j))
    out_shape = jax.ShapeDtypeStruct((n, nc), F32)
    args = [x2, gain.reshape(1, d), w.astype(BF16)]
    kern = _proj_kernel
    if wt is not None:
        rows = wt.shape[0]
        in_specs.append(pl.BlockSpec((rows, d), lambda i, j: (0, 0)))
        out_specs = [out_specs, pl.BlockSpec((rows, tm), lambda i, j: (0, i))]
        out_shape = [out_shape, jax.ShapeDtypeStruct((rows, n), F32)]
        args.append(wt.astype(BF16))
        kern = _proj_t_kernel
    return pl.pallas_call(
        kern,
        grid=(n // tm, nc // tn),
        in_specs=in_specs, out_specs=out_specs, out_shape=out_shape,
        scratch_shapes=[pltpu.VMEM((tm, d), BF16)],
        compiler_params=_params(("parallel", "arbitrary")),
        name="norm_proj",
    )(*args)


def _outproj_kernel(*refs, n_in):
    x_ref = refs[0]
    a_refs = refs[1:1 + n_in]
    w_refs = refs[1 + n_in:1 + 2 * n_in]
    o_ref = refs[1 + 2 * n_in]
    acc = x_ref[...]
    for a_ref, w_ref in zip(a_refs, w_refs):
        acc = acc + _dot(a_ref[...], w_ref[...])
    o_ref[...] = acc


def _out_proj(x2, acts, ws, *, tm):
    n, d = x2.shape
    n_in = len(acts)
    in_specs = [pl.BlockSpec((tm, d), lambda i: (i, 0))]
    in_specs += [pl.BlockSpec((tm, a.shape[1]), lambda i: (i, 0)) for a in acts]
    in_specs += [_resident(w.shape) for w in ws]
    return pl.pallas_call(
        functools.partial(_outproj_kernel, n_in=n_in),
        grid=(n // tm,),
        in_specs=in_specs,
        out_specs=pl.BlockSpec((tm, d), lambda i: (i, 0)),
        out_shape=jax.ShapeDtypeStruct((n, d), F32),
        compiler_params=_params(("parallel",)),
        name="out_proj",
    )(x2, *acts, *[w.astype(BF16) for w in ws])


def _fox_c_kernel(f_ref, b_ref, c_ref, *, seq):
    z = f_ref[...] + b_ref[...]
    logf = jnp.minimum(z, 0.0) - jnp.log1p(jnp.exp(-jnp.abs(z)))
    r = lax.broadcasted_iota(jnp.int32, (LANES, LANES), 0)
    c = lax.broadcasted_iota(jnp.int32, (LANES, LANES), 1)
    upper = (r <= c).astype(F32)
    carry = jnp.zeros((f_ref.shape[0], 1), F32)
    for ch in range(seq // LANES):
        seg = logf[:, ch * LANES:(ch + 1) * LANES]
        cs = jnp.dot(seg, upper, precision=HIGHEST, preferred_element_type=F32) + carry
        c_ref[:, ch * LANES:(ch + 1) * LANES] = cs
        carry = cs[:, LANES - 1:LANES]


def _fox_c(f_t, bias8, *, batch, seq):
    rows = f_t.shape[0]
    return pl.pallas_call(
        functools.partial(_fox_c_kernel, seq=seq),
        grid=(batch,),
        in_specs=[pl.BlockSpec((rows, seq), lambda b: (0, b)),
                  pl.BlockSpec((rows, 1), lambda b: (0, 0))],
        out_specs=pl.BlockSpec((None, rows, seq), lambda b: (b, 0, 0)),
        out_shape=jax.ShapeDtypeStruct((batch, rows, seq), F32),
        compiler_params=_params(("parallel",)),
        name="fox_c",
    )(f_t, bias8)


def _softmax_block(s, m, l, acc, v):
    m_new = jnp.maximum(m, jnp.max(s, axis=-1, keepdims=True))
    alpha = jnp.exp(m - m_new)
    p = jnp.exp(s - m_new)
    l_new = alpha * l + jnp.sum(p, axis=-1, keepdims=True)
    acc_new = alpha * acc + _dot(p, v)
    return m_new, l_new, acc_new


def _softmax_first(s, v):
    m = jnp.max(s, axis=-1, keepdims=True)
    p = jnp.exp(s - m)
    return m, jnp.sum(p, axis=-1, keepdims=True), _dot(p, v)


def _fox_kernel(q_ref, k_ref, v_ref, c_ref, qg_ref, kg_ref, o_ref, kn_ref, vb_ref, *, tq, seq):
    i = pl.program_id(2)
    lo = _lo_mask()

    @pl.when(i == 0)
    def _():
        def prep(j, carry):
            rows = pl.ds(pl.multiple_of(j * tq, tq), tq)
            kn_ref[rows, :] = _pair_rms(k_ref[rows, :], kg_ref[...], lo).astype(BF16)
            vb_ref[rows, :] = v_ref[rows, :].astype(BF16)
            return carry
        lax.fori_loop(0, seq // tq, prep, 0)

    q = _pair_rms(q_ref[...], qg_ref[...], lo)
    qh = (jnp.where(lo, q, 0.0).astype(BF16), jnp.where(lo, 0.0, q).astype(BF16))

    start = pl.multiple_of(i * tq, tq)
    c_q = c_ref[:, pl.ds(start, tq)]
    c_first = c_q[:, 0:1]

    kd = kn_ref[pl.ds(start, tq), :]
    vd = vb_ref[pl.ds(start, tq), :]
    rr = lax.broadcasted_iota(jnp.int32, (tq, tq), 0)
    cc = lax.broadcasted_iota(jnp.int32, (tq, tq), 1)
    causal = cc <= rr
    bias_d = c_first - c_q
    state = []
    for h in range(2):
        s = _dot_nt(qh[h], kd) + bias_d[h:h + 1, :]
        s = jnp.where(causal, s, MASKED)
        state.extend(_softmax_first(s, vd))

    def body(j, st):
        off = pl.multiple_of(j * tq, tq)
        kj = kn_ref[pl.ds(off, tq), :]
        vj = vb_ref[pl.ds(off, tq), :]
        bias = c_first - c_ref[:, pl.ds(off, tq)]
        out = []
        for h in range(2):
            s = _dot_nt(qh[h], kj) + bias[h:h + 1, :]
            out.extend(_softmax_block(s, st[3 * h], st[3 * h + 1], st[3 * h + 2], vj))
        return tuple(out)

    m0, l0, a0, m1, l1, a1 = lax.fori_loop(0, i, body, tuple(state))
    o_ref[...] = jnp.where(lo, a0 / l0, a1 / l1)


def _fox_attention(z, c4, q_gain, k_gain, *, tq):
    batch, seq, _ = z.shape
    n_pairs = c4.shape[1]
    scale = HEAD_DIM ** -0.5
    qg = jnp.tile(q_gain, 2).reshape(1, PAIR) * scale
    kg = jnp.tile(k_gain, 2).reshape(1, PAIR)
    return pl.pallas_call(
        functools.partial(_fox_kernel, tq=tq, seq=seq),
        grid=(batch, n_pairs, seq // tq),
        in_specs=[pl.BlockSpec((None, tq, PAIR), lambda b, p, i: (b, i, p)),
                  pl.BlockSpec((None, seq, PAIR), lambda b, p, i: (b, 0, n_pairs + p)),
                  pl.BlockSpec((None, seq, PAIR), lambda b, p, i: (b, 0, 2 * n_pairs + p)),
                  pl.BlockSpec((None, None, 2, seq), lambda b, p, i: (b, p, 0, 0)),
                  pl.BlockSpec((1, PAIR), lambda b, p, i: (0, 0)),
                  pl.BlockSpec((1, PAIR), lambda b, p, i: (0, 0))],
        out_specs=pl.BlockSpec((None, tq, PAIR), lambda b, p, i: (b, i, p)),
        out_shape=jax.ShapeDtypeStruct((batch, seq, n_pairs * PAIR), F32),
        scratch_shapes=[pltpu.VMEM((seq, PAIR), BF16), pltpu.VMEM((seq, PAIR), BF16)],
        compiler_params=_params(("parallel", "parallel", "arbitrary")),
        name="fox_attn",
    )(z, z, z, c4, qg, kg)


def _rope(x, cos_f, sin_f):
    lane = lax.broadcasted_iota(jnp.int32, (1, PAIR), 1) % HEAD_DIM
    partner = jnp.where(lane < ROPE_HALF,
                        pltpu.roll(x, PAIR - ROPE_HALF, axis=1),
                        pltpu.roll(x, ROPE_HALF, axis=1))
    return x * cos_f + partner * sin_f


def _topk_mask(gate, n_valid):
    lane = lax.broadcasted_iota(jnp.int32, gate.shape, 1)
    lane_f = lane.astype(F32)
    neg = -jnp.inf
    g = jnp.where(lane < n_valid, gate, neg)
    sel = jnp.zeros(gate.shape, F32)
    for _ in range(MOBA_TOPK):
        mx = jnp.max(g, axis=-1, keepdims=True)
        first = jnp.min(jnp.where(g == mx, lane_f, float(LANES)), axis=-1, keepdims=True)
        pick = jnp.logical_and(lane_f == first, mx > neg)
        sel = jnp.where(pick, 1.0, sel)
        g = jnp.where(pick, neg, g)
    return sel


def _moba_kernel(q_ref, k_ref, v_ref, cq_ref, sq_ref, ck_ref, sk_ref, qg_ref, kg_ref,
                 o_ref, kn_ref, vb_ref, km_ref, *, tq, seq):
    i = pl.program_id(2)
    lo = _lo_mask()
    n_blk = seq // tq

    @pl.when(i == 0)
    def _():
        km_ref[...] = jnp.zeros_like(km_ref)
        for n in range(n_blk):
            rows = pl.ds(n * tq, tq)
            kn = _rope(_pair_rms(k_ref[rows, :], kg_ref[...], lo), ck_ref[rows, :], sk_ref[rows, :])
            kn_ref[rows, :] = kn.astype(BF16)
            vb_ref[rows, :] = v_ref[rows, :].astype(BF16)
            km_ref[n:n + 1, :] = jnp.mean(kn, axis=0, keepdims=True)

    q = _rope(_pair_rms(q_ref[...], qg_ref[...], lo), cq_ref[...], sq_ref[...])
    q_lo = jnp.where(lo, q, 0.0)
    q_hi = jnp.where(lo, 0.0, q)
    km = km_ref[...]
    scale = HEAD_DIM ** -0.5
    qh, sel = [], []
    for qm in (q_lo, q_hi):
        gate = lax.dot_general(qm, km, (((1,), (1,)), ((), ())), precision=HIGHEST,
                               preferred_element_type=F32)
        sel.append(_topk_mask(gate, i))
        qh.append((qm * scale).astype(BF16))

    start = pl.multiple_of(i * tq, tq)
    kd = kn_ref[pl.ds(start, tq), :]
    vd = vb_ref[pl.ds(start, tq), :]
    rr = lax.broadcasted_iota(jnp.int32, (tq, tq), 0)
    cc = lax.broadcasted_iota(jnp.int32, (tq, tq), 1)
    causal = cc <= rr
    state = []
    for h in range(2):
        s = jnp.where(causal, _dot_nt(qh[h], kd), MASKED)
        state.extend(_softmax_first(s, vd))

    lane = lax.broadcasted_iota(jnp.int32, (1, LANES), 1)

    def body(j, st):
        off = pl.multiple_of(j * tq, tq)
        kj = kn_ref[pl.ds(off, tq), :]
        vj = vb_ref[pl.ds(off, tq), :]
        out = []
        for h in range(2):
            chosen = jnp.max(jnp.where(lane == j, sel[h], 0.0), axis=-1, keepdims=True)
            s = jnp.where(chosen > 0.0, _dot_nt(qh[h], kj), MASKED)
            out.extend(_softmax_block(s, st[3 * h], st[3 * h + 1], st[3 * h + 2], vj))
        return tuple(out)

    m0, l0, a0, m1, l1, a1 = lax.fori_loop(0, i, body, tuple(state))
    o_ref[...] = jnp.where(lo, a0 / l0, a1 / l1)


def _moba_attention(z, cos_f, sin_f, q_gain, k_gain, *, col0, n_pairs):
    batch, seq, _ = z.shape
    tq = MOBA_BLOCK
    assert seq % tq == 0 and seq // tq <= LANES
    qg = jnp.tile(q_gain, 2).reshape(1, PAIR)
    kg = jnp.tile(k_gain, 2).reshape(1, PAIR)
    tile = pl.BlockSpec((None, tq, PAIR), lambda b, p, i: (b, i, 0))
    full = pl.BlockSpec((None, seq, PAIR), lambda b, p, i: (b, 0, 0))
    return pl.pallas_call(
        functools.partial(_moba_kernel, tq=tq, seq=seq),
        grid=(batch, n_pairs, seq // tq),
        in_specs=[pl.BlockSpec((None, tq, PAIR), lambda b, p, i: (b, i, col0 + p)),
                  pl.BlockSpec((None, seq, PAIR), lambda b, p, i: (b, 0, col0 + n_pairs + p)),
                  pl.BlockSpec((None, seq, PAIR), lambda b, p, i: (b, 0, col0 + 2 * n_pairs + p)),
                  tile, tile, full, full,
                  pl.BlockSpec((1, PAIR), lambda b, p, i: (0, 0)),
                  pl.BlockSpec((1, PAIR), lambda b, p, i: (0, 0))],
        out_specs=pl.BlockSpec((None, tq, PAIR), lambda b, p, i: (b, i, p)),
        out_shape=jax.ShapeDtypeStruct((batch, seq, n_pairs * PAIR), F32),
        scratch_shapes=[pltpu.VMEM((seq, PAIR), BF16), pltpu.VMEM((seq, PAIR), BF16),
                        pltpu.VMEM((LANES, PAIR), F32)],
        compiler_params=_params(("parallel", "parallel", "arbitrary")),
        name="moba_attn",
    )(z, z, z, cos_f, sin_f, cos_f, sin_f, qg, kg)


def _tri_inverse(n_mat, t):
    r = lax.broadcasted_iota(jnp.int32, (t, t), 0)
    c = lax.broadcasted_iota(jnp.int32, (t, t), 1)
    eye = (r == c).astype(F32)
    m = eye + n_mat
    p = n_mat
    for _ in range(int(math.log2(t)) - 1):
        p = _dot(p, p)
        m = m + _dot(m, p)
    return m


def _rwkv_chunk(r, k2, v, kkn, a, lw, s0, lo, t):
    rr = lax.broadcasted_iota(jnp.int32, (t, t), 0)
    cc = lax.broadcasted_iota(jnp.int32, (t, t), 1)
    tril = (cc <= rr).astype(F32)
    cs = jnp.dot(tril, lw, precision=HIGHEST, preferred_element_type=F32)
    c_last = cs[t - 1:t, :]
    e_inc = jnp.exp(cs)
    e_exc = jnp.exp(cs - lw)
    e_inv = jnp.exp(-cs)
    e_rem = jnp.exp(c_last - cs)
    beta = kkn * a
    left = jnp.concatenate([-kkn * e_exc, r * e_inc], axis=0)
    right = jnp.concatenate([beta * e_inv, k2 * e_inv], axis=0)
    right_end = jnp.concatenate([beta * e_rem, k2 * e_rem], axis=0)
    from_state = _dot_nt(left, s0)

    r2 = lax.broadcasted_iota(jnp.int32, (t, 2 * t), 0)
    c2 = lax.broadcasted_iota(jnp.int32, (t, 2 * t), 1)
    strict_k = jnp.logical_and(c2 >= t, c2 - t < r2)
    incl2 = jnp.where(c2 >= t, c2 - t, c2) <= r2
    vv = jnp.concatenate([v, v], axis=0)

    grams, invs = [], []
    for h in range(2):
        lh = jnp.where(lo, left, 0.0) if h == 0 else jnp.where(lo, 0.0, left)
        g = _dot_nt(lh, right)
        grams.append(g)
        n_mat = jnp.where(cc < rr, g[:t, :t], 0.0)
        invs.append(_tri_inverse(n_mat, t))

    x = from_state[:t] + jnp.where(
        lo, _dot(jnp.where(strict_k, grams[0][:t], 0.0), vv),
        _dot(jnp.where(strict_k, grams[1][:t], 0.0), vv))
    u = jnp.where(lo, _dot(invs[0], x), _dot(invs[1], x))
    uv = jnp.concatenate([u, v], axis=0)
    y = from_state[t:] + jnp.where(
        lo, _dot(jnp.where(incl2, grams[0][t:], 0.0), uv),
        _dot(jnp.where(incl2, grams[1][t:], 0.0), uv))

    rs = lax.broadcasted_iota(jnp.int32, (PAIR, PAIR), 0) < HEAD_DIM
    cs_ = lax.broadcasted_iota(jnp.int32, (PAIR, PAIR), 1) < HEAD_DIM
    same_head = rs == cs_
    s_new = jnp.where(same_head, s0 * jnp.exp(c_last) + _dot_tn(uv, right_end), 0.0)
    return y, s_new


def _rwkv_kernel(r_ref, k_ref, v_ref, l_ref, mur_ref, muk_ref, muv_ref, mul_ref,
                 w0_ref, w2_ref, a0_ref, a2_ref, g2_ref, kk_ref, ka_ref, rk_ref,
                 lnw_ref, lnb_ref, o_ref, st_ref, pr_ref, pk_ref, pv_ref, pl_ref, *, tc, t):
    @pl.when(pl.program_id(1) == 0)
    def _():
        st_ref[...] = jnp.zeros_like(st_ref)
        pr_ref[...] = jnp.zeros_like(pr_ref)
        pk_ref[...] = jnp.zeros_like(pk_ref)
        pv_ref[...] = jnp.zeros_like(pv_ref)
        pl_ref[...] = jnp.zeros_like(pl_ref)

    row = lax.broadcasted_iota(jnp.int32, (tc, 1), 0)

    def shifted(x_ref, p_ref, mu_ref):
        x = x_ref[...]
        prev = jnp.where(row == 0, p_ref[...], pltpu.roll(x, 1, axis=0))
        p_ref[...] = x[tc - 1:tc, :]
        return x + (prev - x) * mu_ref[...]

    r = shifted(r_ref, pr_ref, mur_ref)
    k = shifted(k_ref, pk_ref, muk_ref)
    v = shifted(v_ref, pv_ref, muv_ref)
    lora = shifted(l_ref, pl_ref, mul_ref)
    wa = lora[:, :LANES]
    gl = lora[:, LANES:]
    w_raw = w0_ref[...] + _dot(jnp.tanh(wa), w2_ref[...])
    lw = -math.exp(-0.5) * jax.nn.sigmoid(w_raw)
    a = jax.nn.sigmoid(a0_ref[...] + _dot(wa, a2_ref[...]))
    g = _dot(jax.nn.sigmoid(gl), g2_ref[...])
    kk = k * kk_ref[...]
    k2 = k * (1.0 + (a - 1.0) * ka_ref[...])

    lo = _lo_mask()
    for p in range(r.shape[1] // PAIR):
        cols = slice(p * PAIR, (p + 1) * PAIR)
        kk_p = kk[:, cols]
        nrm = jnp.sqrt(_halfsum(kk_p * kk_p, lo))
        kkn_p = kk_p / jnp.maximum(nrm, 1e-12)
        r_p, k2_p, v_p, a_p, lw_p = r[:, cols], k2[:, cols], v[:, cols], a[:, cols], lw[:, cols]
        bonus = _halfsum(r_p * k2_p * rk_ref[:, cols], lo) * v_p
        s = st_ref[p]
        for ch in range(tc // t):
            rows = slice(ch * t, (ch + 1) * t)
            y, s = _rwkv_chunk(r_p[rows], k2_p[rows], v_p[rows], kkn_p[rows], a_p[rows],
                               lw_p[rows], s, lo, t)
            mean = _halfsum(y, lo) * (1.0 / HEAD_DIM)
            d = y - mean
            var = _halfsum(d * d, lo) * (1.0 / HEAD_DIM)
            yn = d * lax.rsqrt(var + RWKV_GN_EPS) * lnw_ref[:, cols] + lnb_ref[:, cols]
            o_ref[rows, cols] = (yn + bonus[rows]) * g[rows, cols]
        st_ref[p] = s


def _rwkv(z, prm, *, col_r, col_l, width, lora_w, tc):
    batch, seq, _ = z.shape
    n_pairs = width // PAIR
    blk = lambda c: pl.BlockSpec((None, tc, width), lambda b, i: (b, i, c))
    vec = lambda w: pl.BlockSpec((1, w), lambda b, i: (0, 0))
    mat = lambda s: pl.BlockSpec(s, lambda b, i: (0, 0))
    in_specs = [blk(col_r), blk(col_r + 1), blk(col_r + 2),
                pl.BlockSpec((None, tc, lora_w), lambda b, i: (b, i, col_l)),
                vec(width), vec(width), vec(width), vec(lora_w),
                vec(width), mat((LANES, width)), vec(width), mat((LANES, width)),
                mat((lora_w - LANES, width)),
                vec(width), vec(width), vec(width), vec(width), vec(width)]
    return pl.pallas_call(
        functools.partial(_rwkv_kernel, tc=tc, t=RWKV_CHUNK),
        grid=(batch, seq // tc),
        in_specs=in_specs,
        out_specs=pl.BlockSpec((None, tc, width), lambda b, i: (b, i, 0)),
        out_shape=jax.ShapeDtypeStruct((batch, seq, width), F32),
        scratch_shapes=[pltpu.VMEM((n_pairs, PAIR, PAIR), F32),
                        pltpu.VMEM((1, width), F32), pltpu.VMEM((1, width), F32),
                        pltpu.VMEM((1, width), F32), pltpu.VMEM((1, lora_w), F32)],
        compiler_params=_params(("parallel", "arbitrary")),
        name="rwkv7",
    )(z, z, z, z, *prm)


def _xattn_kernel(x_ref, kv_ref, g_ref, wq_ref, wo_ref, qg_ref, kg_ref, o_ref, kn_ref, vb_ref):
    width = XATTN_HEADS * XATTN_HEAD_DIM

    @pl.when(pl.program_id(1) == 0)
    def _():
        for h in range(XATTN_HEADS):
            cols = slice(h * XATTN_HEAD_DIM, (h + 1) * XATTN_HEAD_DIM)
            kn_ref[:, cols] = _rms(kv_ref[:, cols], kg_ref[...]).astype(BF16)
        vb_ref[...] = kv_ref[:, width:].astype(BF16)

    x = x_ref[...]
    q = _dot(_rms(x, g_ref[...]), wq_ref[...])
    scale = XATTN_HEAD_DIM ** -0.5
    outs = []
    for h in range(XATTN_HEADS):
        cols = slice(h * XATTN_HEAD_DIM, (h + 1) * XATTN_HEAD_DIM)
        qh = _rms(q[:, cols], qg_ref[...])
        s = _dot_nt(qh, kn_ref[:, cols]) * scale
        m = jnp.max(s, axis=-1, keepdims=True)
        p = jnp.exp(s - m)
        l = jnp.sum(p, axis=-1, keepdims=True)
        outs.append(_dot(p / l, vb_ref[:, cols]))
    o_ref[...] = x + _dot(jnp.concatenate(outs, axis=-1), wo_ref[...])


def _xattn(x, kv, gain, w_q, w_o, q_gain, k_gain, *, tq):
    batch, seq, d = x.shape
    mem_len = kv.shape[1]
    width = XATTN_HEADS * XATTN_HEAD_DIM
    return pl.pallas_call(
        _xattn_kernel,
        grid=(batch, seq // tq),
        in_specs=[pl.BlockSpec((None, tq, d), lambda b, i: (b, i, 0)),
                  pl.BlockSpec((None, mem_len, 2 * width), lambda b, i: (b, 0, 0)),
                  _resident((1, d)), _resident((d, width)), _resident((width, d)),
                  _resident((1, XATTN_HEAD_DIM)), _resident((1, XATTN_HEAD_DIM))],
        out_specs=pl.BlockSpec((None, tq, d), lambda b, i: (b, i, 0)),
        out_shape=jax.ShapeDtypeStruct((batch, seq, d), F32),
        scratch_shapes=[pltpu.VMEM((mem_len, width), BF16), pltpu.VMEM((mem_len, width), BF16)],
        compiler_params=_params(("parallel", "arbitrary")),
        name="xattn",
    )(x, kv, gain.reshape(1, d), w_q.astype(BF16), w_o.astype(BF16),
      q_gain.reshape(1, -1), k_gain.reshape(1, -1))


def _rope_tables(positions):
    inv_freq = ROPE_THETA ** (-jnp.arange(0, ROPE_DIM, 2, dtype=F32) / ROPE_DIM)
    ang = positions.astype(F32)[..., None] * inv_freq
    cos, sin = jnp.cos(ang), jnp.sin(ang)
    rest = HEAD_DIM - ROPE_DIM
    ones = jnp.ones(cos.shape[:-1] + (rest,), F32)
    zeros = jnp.zeros(cos.shape[:-1] + (rest,), F32)
    cos_f = jnp.concatenate([cos, cos, ones], axis=-1)
    sin_f = jnp.concatenate([-sin, sin, zeros], axis=-1)
    return jnp.tile(cos_f, 2), jnp.tile(sin_f, 2)


def _pad_rows(w, rows, at):
    out = jnp.zeros((rows, w.shape[1]), w.dtype)
    return lax.dynamic_update_slice(out, w, (at, 0))


def _mixers(x, cos_f, sin_f, mix_norm, mix_w_in, fox_f_bias, fox_q_gain, fox_k_gain,
            moba_q_gain, moba_k_gain, rwkv_mu, rwkv_w0, rwkv_w2, rwkv_a0, rwkv_a2, rwkv_g2,
            rwkv_k_k, rwkv_k_a, rwkv_r_k, rwkv_ln_w, rwkv_ln_b, *, tm):
    batch, seq, d = x.shape
    n = batch * seq
    fox_heads = fox_f_bias.shape[0]
    fox_w = fox_heads * HEAD_DIM
    moba_w = fox_w
    rwkv_w = rwkv_w0.shape[0]
    dec_l, icl_l, gate_l = rwkv_w2.shape[0], rwkv_a2.shape[0], rwkv_g2.shape[0]
    assert fox_w % PAIR == 0 and rwkv_w % PAIR == 0 and dec_l + icl_l == LANES
    fox_in = 3 * fox_w + fox_heads
    moba_in = 3 * moba_w
    lora_in = dec_l + icl_l + gate_l
    lora_w = LANES * (-(-(lora_in + fox_heads) // LANES))
    in_width = mix_w_in.shape[1]
    head = 3 * fox_w + moba_in + 3 * rwkv_w
    assert head % lora_w == 0 and head % rwkv_w == 0
    r0 = fox_in + moba_in
    w_perm = jnp.concatenate([
        mix_w_in[:, :3 * fox_w], mix_w_in[:, fox_in:fox_in + moba_in],
        mix_w_in[:, r0:r0 + 3 * rwkv_w], mix_w_in[:, r0 + 3 * rwkv_w:in_width],
        mix_w_in[:, 3 * fox_w:fox_in],
        jnp.zeros((d, lora_w - lora_in - fox_heads), F32)], axis=1)
    total = w_perm.shape[1]
    wf_t = _pad_rows(mix_w_in[:, 3 * fox_w:fox_in].T, 8, 0)

    x2 = x.reshape(n, d)
    tn = total // 3 if total % (3 * LANES) == 0 else LANES
    z2, f_t = _norm_proj(x2, mix_norm, w_perm, tm=tm, tn=tn, wt=wf_t)
    z = z2.reshape(batch, seq, total)

    bias8 = _pad_rows(fox_f_bias.reshape(fox_heads, 1), 8, 0)
    c = _fox_c(f_t, bias8, batch=batch, seq=seq)
    c4 = c[:, :fox_heads].reshape(batch, fox_heads // 2, 2, seq)
    y_fox = _fox_attention(z, c4, fox_q_gain, fox_k_gain, tq=256)
    y_moba = _moba_attention(z, cos_f, sin_f, moba_q_gain, moba_k_gain,
                             col0=3 * fox_w // PAIR, n_pairs=moba_w // PAIR)

    mu_l = jnp.concatenate([rwkv_mu[3 * rwkv_w:], jnp.zeros((lora_w - lora_in,), F32)])
    prm = [rwkv_mu[:rwkv_w], rwkv_mu[rwkv_w:2 * rwkv_w], rwkv_mu[2 * rwkv_w:3 * rwkv_w], mu_l,
           rwkv_w0, _pad_rows(rwkv_w2, LANES, 0).astype(BF16),
           rwkv_a0, _pad_rows(rwkv_a2, LANES, dec_l).astype(BF16),
           _pad_rows(rwkv_g2, lora_w - LANES, 0).astype(BF16),
           rwkv_k_k, rwkv_k_a, rwkv_r_k.reshape(-1), rwkv_ln_w, rwkv_ln_b]
    prm = [p.reshape(1, -1) if p.ndim == 1 else p for p in prm]
    y_rwkv = _rwkv(z, prm, col_r=(3 * fox_w + moba_in) // rwkv_w, col_l=head // lora_w,
                   width=rwkv_w, lora_w=lora_w, tc=256)

    return y_fox, y_rwkv, y_moba


def _mixer_layer(x, cos_f, sin_f, mix_norm, mix_w_in, mix_w_out, *mixer_params, tm):
    batch, seq, d = x.shape
    n = batch * seq
    ys = _mixers(x, cos_f, sin_f, mix_norm, mix_w_in, *mixer_params, tm=tm)
    acts = [y.reshape(n, y.shape[-1]) for y in ys]
    bounds = [0]
    for a in acts:
        bounds.append(bounds[-1] + a.shape[1])
    ws = [mix_w_out[lo:hi] for lo, hi in zip(bounds[:-1], bounds[1:])]
    return _out_proj(x.reshape(n, d), acts, ws, tm=tm).reshape(batch, seq, d)


def kernel(x, mem, positions, ffn1_norm, ffn1_w_in, ffn1_w_out, mix_norm, mix_w_in, mix_w_out, fox_f_bias, fox_q_gain, fox_k_gain, moba_q_gain, moba_k_gain, rwkv_mu, rwkv_w0, rwkv_w2, rwkv_a0, rwkv_a2, rwkv_g2, rwkv_k_k, rwkv_k_a, rwkv_r_k, rwkv_ln_w, rwkv_ln_b, xattn_norm, xattn_mem_norm, xattn_w_q, xattn_w_kv, xattn_q_gain, xattn_k_gain, xattn_w_out, ffn2_norm, ffn2_w_in, ffn2_w_out):
    batch, seq, d = x.shape
    n = batch * seq
    depth = ffn1_norm.shape[0]
    tm = 512 if n % 512 == 0 else 256
    tf = 256
    cos_f, sin_f = _rope_tables(positions)
    mem2 = mem.reshape(-1, d)
    for l in range(depth):
        x = _ffn(x.reshape(n, d), ffn1_norm[l], ffn1_w_in[l], ffn1_w_out[l], tm=tm, tf=tf)
        x = x.reshape(batch, seq, d)
        x = _mixer_layer(x, cos_f, sin_f, mix_norm[l], mix_w_in[l], mix_w_out[l], fox_f_bias[l],
                         fox_q_gain[l], fox_k_gain[l], moba_q_gain[l], moba_k_gain[l], rwkv_mu[l],
                         rwkv_w0[l], rwkv_w2[l], rwkv_a0[l], rwkv_a2[l], rwkv_g2[l], rwkv_k_k[l],
                         rwkv_k_a[l], rwkv_r_k[l], rwkv_ln_w[l], rwkv_ln_b[l], tm=tm)
        kv = _norm_proj(mem2, xattn_mem_norm[l], xattn_w_kv[l], tm=256, tn=512)
        x = _xattn(x, kv.reshape(batch, -1, kv.shape[-1]), xattn_norm[l], xattn_w_q[l],
                   xattn_w_out[l], xattn_q_gain[l], xattn_k_gain[l], tq=512 if seq % 512 == 0 else 256)
        x = _ffn(x.reshape(n, d), ffn2_norm[l], ffn2_w_in[l], ffn2_w_out[l], tm=tm, tf=tf)
        x = x.reshape(batch, seq, d)
    return x
```

```python
import functools
import math

import jax
import jax.numpy as jnp
from jax import lax
from jax.experimental import pallas as pl
from jax.experimental.pallas import tpu as pltpu

F32 = jnp.float32
BF16 = jnp.bfloat16
HIGHEST = lax.Precision.HIGHEST

HEAD_DIM = 64
PAIR = 2 * HEAD_DIM
LANES = 128
NORM_EPS = 1e-6
RWKV_GN_EPS = 64e-5
MOBA_BLOCK = 256
MOBA_TOPK = 3
ROPE_THETA = 500000.0
ROPE_DIM = HEAD_DIM // 4
ROPE_HALF = ROPE_DIM // 2
XATTN_HEADS = 4
XATTN_HEAD_DIM = 128
RWKV_CHUNK = 64
MASKED = -1e30
KEY_SPAN = 1024
VMEM_LIMIT = 56 * 1024 * 1024


def _params(sem):
    return pltpu.CompilerParams(dimension_semantics=sem, vmem_limit_bytes=VMEM_LIMIT)


def _dot(a, b):
    return jnp.dot(a.astype(BF16), b.astype(BF16), preferred_element_type=F32)


def _dot_nt(a, b):
    return lax.dot_general(a.astype(BF16), b.astype(BF16), (((1,), (1,)), ((), ())),
                           preferred_element_type=F32)


def _dot_tn(a, b):
    return lax.dot_general(a.astype(BF16), b.astype(BF16), (((0,), (0,)), ((), ())),
                           preferred_element_type=F32)


def _resident(shape):
    nd = len(shape)
    return pl.BlockSpec(shape, lambda *_: (0,) * nd, pipeline_mode=pl.Buffered(1))


def _rms(x, gain):
    ms = jnp.mean(x * x, axis=-1, keepdims=True)
    return x * lax.rsqrt(ms + NORM_EPS) * gain


def _lo_mask():
    return lax.broadcasted_iota(jnp.int32, (1, PAIR), 1) < HEAD_DIM


def _halfsum(x, lo):
    s0 = jnp.sum(jnp.where(lo, x, 0.0), axis=-1, keepdims=True)
    s1 = jnp.sum(jnp.where(lo, 0.0, x), axis=-1, keepdims=True)
    return jnp.where(lo, s0, s1)


def _pair_rms(x, gain, lo):
    ms = _halfsum(x * x, lo) * (1.0 / HEAD_DIM)
    return x * lax.rsqrt(ms + NORM_EPS) * gain


def _ffn_kernel(x_ref, g_ref, wg_ref, wu_ref, wo_ref, o_ref, h_ref, acc_ref, *, nf):
    h_ref[...] = _rms(x_ref[...], g_ref[...]).astype(BF16)
    acc_ref[...] = jnp.zeros_like(acc_ref)

    def body(f, carry):
        h = h_ref[...]
        gate = jnp.dot(h, wg_ref[f], preferred_element_type=F32)
        up = jnp.dot(h, wu_ref[f], preferred_element_type=F32)
        act = (gate * jax.nn.sigmoid(gate) * up).astype(BF16)
        acc_ref[...] += jnp.dot(act, wo_ref[f], preferred_element_type=F32)
        return carry

    lax.fori_loop(0, nf, body, 0)
    o_ref[...] = x_ref[...] + 0.5 * acc_ref[...]


def _ffn(x2, gain, w_in, w_out, *, tm, tf):
    n, d = x2.shape
    ffn = w_out.shape[0]
    nf = ffn // tf
    wg = w_in[:, :ffn].astype(BF16).reshape(d, nf, tf).transpose(1, 0, 2)
    wu = w_in[:, ffn:].astype(BF16).reshape(d, nf, tf).transpose(1, 0, 2)
    wo = w_out.astype(BF16).reshape(nf, tf, d)
    return pl.pallas_call(
        functools.partial(_ffn_kernel, nf=nf),
        grid=(n // tm,),
        in_specs=[pl.BlockSpec((tm, d), lambda i: (i, 0)),
                  _resident((1, d)), _resident((nf, d, tf)), _resident((nf, d, tf)),
                  _resident((nf, tf, d))],
        out_specs=pl.BlockSpec((tm, d), lambda i: (i, 0)),
        out_shape=jax.ShapeDtypeStruct((n, d), F32),
        scratch_shapes=[pltpu.VMEM((tm, d), BF16), pltpu.VMEM((tm, d), F32)],
        compiler_params=_params(("parallel",)),
        name="ffn",
    )(x2, gain.reshape(1, d), wg, wu, wo)


def _proj_kernel(x_ref, g_ref, w_ref, o_ref, h_ref):
    @pl.when(pl.program_id(1) == 0)
    def _():
        h_ref[...] = _rms(x_ref[...], g_ref[...]).astype(BF16)

    o_ref[...] = jnp.dot(h_ref[...], w_ref[...], preferred_element_type=F32)


def _proj_t_kernel(x_ref, g_ref, w_ref, wt_ref, o_ref, ot_ref, h_ref):
    @pl.when(pl.program_id(1) == 0)
    def _():
        h_ref[...] = _rms(x_ref[...], g_ref[...]).astype(BF16)
        ot_ref[...] = _dot_nt(wt_ref[...], h_ref[...])

    o_ref[...] = jnp.dot(h_ref[...], w_ref[...], preferred_element_type=F32)


def _norm_proj(x2, gain, w, *, tm, tn, wt=None):
    n, d = x2.shape
    nc = w.shape[1]
    in_specs = [pl.BlockSpec((tm, d), lambda i, j: (i, 0)),
                pl.BlockSpec((1, d), lambda i, j: (0, 0)),
                pl.BlockSpec((d, tn), lambda i, j: (0, j))]
    out_specs = pl.BlockSpec((tm, tn), lambda i, j: (i, j))
    out_shape = jax.ShapeDtypeStruct((n, nc), F32)
    args = [x2, gain.reshape(1, d), w.astype(BF16)]
    kern = _proj_kernel
    if wt is not None:
        rows = wt.shape[0]
        in_specs.append(pl.BlockSpec((rows, d), lambda i, j: (0, 0)))
        out_specs = [out_specs, pl.BlockSpec((rows, tm), lambda i, j: (0, i))]
        out_shape = [out_shape, jax.ShapeDtypeStruct((rows, n), F32)]
        args.append(wt.astype(BF16))
        kern = _proj_t_kernel
    return pl.pallas_call(
        kern,
        grid=(n // tm, nc // tn),
        in_specs=in_specs, out_specs=out_specs, out_shape=out_shape,
        scratch_shapes=[pltpu.VMEM((tm, d), BF16)],
        compiler_params=_params(("parallel", "arbitrary")),
        name="norm_proj",
    )(*args)


def _outproj_kernel(*refs, n_in):
    x_ref = refs[0]
    a_refs = refs[1:1 + n_in]
    w_refs = refs[1 + n_in:1 + 2 * n_in]
    o_ref = refs[1 + 2 * n_in]
    acc = x_ref[...]
    for a_ref, w_ref in zip(a_refs, w_refs):
        acc = acc + _dot(a_ref[...], w_ref[...])
    o_ref[...] = acc


def _out_proj(x2, acts, ws, *, tm):
    n, d = x2.shape
    n_in = len(acts)
    in_specs = [pl.BlockSpec((tm, d), lambda i: (i, 0))]
    in_specs += [pl.BlockSpec((tm, a.shape[1]), lambda i: (i, 0)) for a in acts]
    in_specs += [_resident(w.shape) for w in ws]
    return pl.pallas_call(
        functools.partial(_outproj_kernel, n_in=n_in),
        grid=(n // tm,),
        in_specs=in_specs,
        out_specs=pl.BlockSpec((tm, d), lambda i: (i, 0)),
        out_shape=jax.ShapeDtypeStruct((n, d), F32),
        compiler_params=_params(("parallel",)),
        name="out_proj",
    )(x2, *acts, *[w.astype(BF16) for w in ws])


def _fox_c_kernel(f_ref, b_ref, c_ref, *, seq):
    z = f_ref[...] + b_ref[...]
    logf = jnp.minimum(z, 0.0) - jnp.log1p(jnp.exp(-jnp.abs(z)))
    r = lax.broadcasted_iota(jnp.int32, (LANES, LANES), 0)
    c = lax.broadcasted_iota(jnp.int32, (LANES, LANES), 1)
    upper = (r <= c).astype(F32)
    carry = jnp.zeros((f_ref.shape[0], 1), F32)
    for ch in range(seq // LANES):
        seg = logf[:, ch * LANES:(ch + 1) * LANES]
        cs = jnp.dot(seg, upper, precision=HIGHEST, preferred_element_type=F32) + carry
        c_ref[:, ch * LANES:(ch + 1) * LANES] = cs
        carry = cs[:, LANES - 1:LANES]


def _fox_c(f_t, bias8, *, batch, seq):
    rows = f_t.shape[0]
    return pl.pallas_call(
        functools.partial(_fox_c_kernel, seq=seq),
        grid=(batch,),
        in_specs=[pl.BlockSpec((rows, seq), lambda b: (0, b)),
                  pl.BlockSpec((rows, 1), lambda b: (0, 0))],
        out_specs=pl.BlockSpec((None, rows, seq), lambda b: (b, 0, 0)),
        out_shape=jax.ShapeDtypeStruct((batch, rows, seq), F32),
        compiler_params=_params(("parallel",)),
        name="fox_c",
    )(f_t, bias8)


def _softmax_block(s, m, l, acc, v):
    m_new = jnp.maximum(m, jnp.max(s, axis=-1, keepdims=True))
    alpha = jnp.exp(m - m_new)
    p = jnp.exp(s - m_new)
    l_new = alpha * l + jnp.sum(p, axis=-1, keepdims=True)
    acc_new = alpha * acc + _dot(p, v)
    return m_new, l_new, acc_new


def _softmax_first(s, v):
    m = jnp.max(s, axis=-1, keepdims=True)
    p = jnp.exp(s - m)
    return m, jnp.sum(p, axis=-1, keepdims=True), _dot(p, v)


def _fox_kernel(q_ref, k_ref, v_ref, c_ref, qg_ref, kg_ref, o_ref, kn_ref, vb_ref, *, tq, seq, tks):
    i = pl.program_id(2)
    lo = _lo_mask()

    @pl.when(i == 0)
    def _():
        def prep(j, carry):
            rows = pl.ds(pl.multiple_of(j * tq, tq), tq)
            kn_ref[rows, :] = _pair_rms(k_ref[rows, :], kg_ref[...], lo).astype(BF16)
            vb_ref[rows, :] = v_ref[rows, :].astype(BF16)
            return carry
        lax.fori_loop(0, seq // tq, prep, 0)

    q = _pair_rms(q_ref[...], qg_ref[...], lo)
    qh = (jnp.where(lo, q, 0.0).astype(BF16), jnp.where(lo, 0.0, q).astype(BF16))

    start = pl.multiple_of(i * tq, tq)
    c_first = c_ref[:, pl.ds(start, tq)][:, 0:1]
    n_full = (i * tq) // tks
    rem = pl.multiple_of(n_full * tks, tks)

    k_r = kn_ref[pl.ds(rem, tks), :]
    v_r = vb_ref[pl.ds(rem, tks), :]
    q_pos = start + lax.broadcasted_iota(jnp.int32, (tq, 1), 0)
    k_pos = rem + lax.broadcasted_iota(jnp.int32, (1, tks), 1)
    visible = k_pos <= q_pos
    bias_r = c_first - c_ref[:, pl.ds(rem, tks)]
    state = []
    for h in range(2):
        s = _dot_nt(qh[h], k_r) + bias_r[h:h + 1, :]
        state.extend(_softmax_first(jnp.where(visible, s, MASKED), v_r))

    def body(j, st):
        off = pl.multiple_of(j * tks, tks)
        kj = kn_ref[pl.ds(off, tks), :]
        vj = vb_ref[pl.ds(off, tks), :]
        bias = c_first - c_ref[:, pl.ds(off, tks)]
        out = []
        for h in range(2):
            s = _dot_nt(qh[h], kj) + bias[h:h + 1, :]
            out.extend(_softmax_block(s, st[3 * h], st[3 * h + 1], st[3 * h + 2], vj))
        return tuple(out)

    m0, l0, a0, m1, l1, a1 = lax.fori_loop(0, n_full, body, tuple(state))
    o_ref[...] = jnp.where(lo, a0 / l0, a1 / l1)


def _fox_attention(z, c4, q_gain, k_gain, *, tq):
    batch, seq, _ = z.shape
    n_pairs = c4.shape[1]
    scale = HEAD_DIM ** -0.5
    qg = jnp.tile(q_gain, 2).reshape(1, PAIR) * scale
    kg = jnp.tile(k_gain, 2).reshape(1, PAIR)
    tks = min(KEY_SPAN, seq)
    assert seq % tks == 0 and tks % tq == 0
    return pl.pallas_call(
        functools.partial(_fox_kernel, tq=tq, seq=seq, tks=tks),
        grid=(batch, n_pairs, seq // tq),
        in_specs=[pl.BlockSpec((None, tq, PAIR), lambda b, p, i: (b, i, p)),
                  pl.BlockSpec((None, seq, PAIR), lambda b, p, i: (b, 0, n_pairs + p)),
                  pl.BlockSpec((None, seq, PAIR), lambda b, p, i: (b, 0, 2 * n_pairs + p)),
                  pl.BlockSpec((None, None, 2, seq), lambda b, p, i: (b, p, 0, 0)),
                  pl.BlockSpec((1, PAIR), lambda b, p, i: (0, 0)),
                  pl.BlockSpec((1, PAIR), lambda b, p, i: (0, 0))],
        out_specs=pl.BlockSpec((None, tq, PAIR), lambda b, p, i: (b, i, p)),
        out_shape=jax.ShapeDtypeStruct((batch, seq, n_pairs * PAIR), F32),
        scratch_shapes=[pltpu.VMEM((seq, PAIR), BF16), pltpu.VMEM((seq, PAIR), BF16)],
        compiler_params=_params(("parallel", "parallel", "arbitrary")),
        name="fox_attn",
    )(z, z, z, c4, qg, kg)


def _rope(x, cos_f, sin_f):
    lane = lax.broadcasted_iota(jnp.int32, (1, PAIR), 1) % HEAD_DIM
    partner = jnp.where(lane < ROPE_HALF,
                        pltpu.roll(x, PAIR - ROPE_HALF, axis=1),
                        pltpu.roll(x, ROPE_HALF, axis=1))
    return x * cos_f + partner * sin_f


def _topk_mask(gate, n_valid):
    lane = lax.broadcasted_iota(jnp.int32, gate.shape, 1)
    lane_f = lane.astype(F32)
    neg = -jnp.inf
    g = jnp.where(lane < n_valid, gate, neg)
    sel = jnp.zeros(gate.shape, F32)
    for _ in range(MOBA_TOPK):
        mx = jnp.max(g, axis=-1, keepdims=True)
        first = jnp.min(jnp.where(g == mx, lane_f, float(LANES)), axis=-1, keepdims=True)
        pick = jnp.logical_and(lane_f == first, mx > neg)
        sel = jnp.where(pick, 1.0, sel)
        g = jnp.where(pick, neg, g)
    return sel


def _moba_kernel(q_ref, k_ref, v_ref, cq_ref, sq_ref, ck_ref, sk_ref, qg_ref, kg_ref,
                 o_ref, kn_ref, vb_ref, km_ref, *, tq, seq, tks):
    i = pl.program_id(2)
    lo = _lo_mask()
    n_blk = seq // tq

    @pl.when(i == 0)
    def _():
        km_ref[...] = jnp.zeros_like(km_ref)
        for n in range(n_blk):
            rows = pl.ds(n * tq, tq)
            kn = _rope(_pair_rms(k_ref[rows, :], kg_ref[...], lo), ck_ref[rows, :], sk_ref[rows, :])
            kn_ref[rows, :] = kn.astype(BF16)
            vb_ref[rows, :] = v_ref[rows, :].astype(BF16)
            km_ref[n:n + 1, :] = jnp.mean(kn, axis=0, keepdims=True)

    q = _rope(_pair_rms(q_ref[...], qg_ref[...], lo), cq_ref[...], sq_ref[...])
    q_lo = jnp.where(lo, q, 0.0)
    q_hi = jnp.where(lo, 0.0, q)
    km = km_ref[...]
    scale = HEAD_DIM ** -0.5
    qh, sel = [], []
    for qm in (q_lo, q_hi):
        gate = lax.dot_general(qm, km, (((1,), (1,)), ((), ())), precision=HIGHEST,
                               preferred_element_type=F32)
        sel.append(_topk_mask(gate, i))
        qh.append((qm * scale).astype(BF16))

    per_span = tks // tq
    n_full = i // per_span
    lane = lax.broadcasted_iota(jnp.int32, (1, LANES), 1)
    rr = lax.broadcasted_iota(jnp.int32, (tq, 1), 0)
    cc = lax.broadcasted_iota(jnp.int32, (1, tq), 1)

    def masked_scores(h, span, k_span, with_own):
        s = _dot_nt(qh[h], k_span)
        parts = []
        for b in range(per_span):
            blk = span * per_span + b
            chosen = jnp.max(jnp.where(lane == blk, sel[h], 0.0), axis=-1, keepdims=True)
            keep = chosen > 0.0
            if with_own:
                own = (blk == i).astype(jnp.int32)
                last_visible = (rr + 1) * own - 1
                keep = jnp.logical_or(keep, cc <= last_visible)
            parts.append(jnp.where(keep, s[:, b * tq:(b + 1) * tq], MASKED))
        return jnp.concatenate(parts, axis=1)

    rem = pl.multiple_of(n_full * tks, tks)
    k_r = kn_ref[pl.ds(rem, tks), :]
    v_r = vb_ref[pl.ds(rem, tks), :]
    state = []
    for h in range(2):
        state.extend(_softmax_first(masked_scores(h, n_full, k_r, True), v_r))

    def body(j, st):
        off = pl.multiple_of(j * tks, tks)
        kj = kn_ref[pl.ds(off, tks), :]
        vj = vb_ref[pl.ds(off, tks), :]
        out = []
        for h in range(2):
            s = masked_scores(h, j, kj, False)
            out.extend(_softmax_block(s, st[3 * h], st[3 * h + 1], st[3 * h + 2], vj))
        return tuple(out)

    m0, l0, a0, m1, l1, a1 = lax.fori_loop(0, n_full, body, tuple(state))
    o_ref[...] = jnp.where(lo, a0 / l0, a1 / l1)


def _moba_attention(z, cos_f, sin_f, q_gain, k_gain, *, col0, n_pairs):
    batch, seq, _ = z.shape
    tq = MOBA_BLOCK
    assert seq % tq == 0 and seq // tq <= LANES
    qg = jnp.tile(q_gain, 2).reshape(1, PAIR)
    kg = jnp.tile(k_gain, 2).reshape(1, PAIR)
    tile = pl.BlockSpec((None, tq, PAIR), lambda b, p, i: (b, i, 0))
    full = pl.BlockSpec((None, seq, PAIR), lambda b, p, i: (b, 0, 0))
    tks = min(KEY_SPAN, seq)
    assert seq % tks == 0 and tks % tq == 0
    return pl.pallas_call(
        functools.partial(_moba_kernel, tq=tq, seq=seq, tks=tks),
        grid=(batch, n_pairs, seq // tq),
        in_specs=[pl.BlockSpec((None, tq, PAIR), lambda b, p, i: (b, i, col0 + p)),
                  pl.BlockSpec((None, seq, PAIR), lambda b, p, i: (b, 0, col0 + n_pairs + p)),
                  pl.BlockSpec((None, seq, PAIR), lambda b, p, i: (b, 0, col0 + 2 * n_pairs + p)),
                  tile, tile, full, full,
                  pl.BlockSpec((1, PAIR), lambda b, p, i: (0, 0)),
                  pl.BlockSpec((1, PAIR), lambda b, p, i: (0, 0))],
        out_specs=pl.BlockSpec((None, tq, PAIR), lambda b, p, i: (b, i, p)),
        out_shape=jax.ShapeDtypeStruct((batch, seq, n_pairs * PAIR), F32),
        scratch_shapes=[pltpu.VMEM((seq, PAIR), BF16), pltpu.VMEM((seq, PAIR), BF16),
                        pltpu.VMEM((LANES, PAIR), F32)],
        compiler_params=_params(("parallel", "parallel", "arbitrary")),
        name="moba_attn",
    )(z, z, z, cos_f, sin_f, cos_f, sin_f, qg, kg)


def _tri_inverses(n_mats, t):
    r = lax.broadcasted_iota(jnp.int32, (t, t), 0)
    c = lax.broadcasted_iota(jnp.int32, (t, t), 1)
    eye = (r == c).astype(F32)
    ms = [eye + n for n in n_mats]
    ps = list(n_mats)
    for _ in range(int(math.log2(t)) - 1):
        ps = [_dot(p, p) for p in ps]
        ms = [m + _dot(m, p) for m, p in zip(ms, ps)]
    return ms


def _headsum(x, lo):
    return jnp.concatenate([_halfsum(x[:, c:c + PAIR], lo) for c in range(0, x.shape[1], PAIR)],
                           axis=1)


def _rwkv_tile(r, k2, v, kkn, a, lw, st_ref, lo, t):
    tc, width = r.shape
    n_ch, n_pairs = tc // t, width // PAIR
    rr = lax.broadcasted_iota(jnp.int32, (t, t), 0)
    cc = lax.broadcasted_iota(jnp.int32, (t, t), 1)
    tril = (cc <= rr).astype(F32)
    r2 = lax.broadcasted_iota(jnp.int32, (t, 2 * t), 0)
    c2 = lax.broadcasted_iota(jnp.int32, (t, 2 * t), 1)
    strict_k = jnp.logical_and(c2 >= t, c2 - t < r2)
    incl2 = jnp.where(c2 >= t, c2 - t, c2) <= r2
    same_head = (lax.broadcasted_iota(jnp.int32, (PAIR, PAIR), 0) < HEAD_DIM) == (
        lax.broadcasted_iota(jnp.int32, (PAIR, PAIR), 1) < HEAD_DIM)
    beta = kkn * a
    units = [(ch, p) for ch in range(n_ch) for p in range(n_pairs)]

    left, right, right_end, decay_end, vs = [], [], [], [], []
    for ch in range(n_ch):
        rows = slice(ch * t, (ch + 1) * t)
        cs = jnp.dot(tril, lw[rows], precision=HIGHEST, preferred_element_type=F32)
        c_last = cs[t - 1:t, :]
        e_inc, e_exc, e_inv, e_rem = (jnp.exp(cs), jnp.exp(cs - lw[rows]), jnp.exp(-cs),
                                      jnp.exp(c_last - cs))
        left.append(jnp.concatenate([-kkn[rows] * e_exc, r[rows] * e_inc], axis=0))
        right.append(jnp.concatenate([beta[rows] * e_inv, k2[rows] * e_inv], axis=0))
        right_end.append(jnp.concatenate([beta[rows] * e_rem, k2[rows] * e_rem], axis=0))
        decay_end.append(jnp.exp(c_last))
        vs.append(v[rows])

    def pc(x, p):
        return x[:, p * PAIR:(p + 1) * PAIR]

    grams = {}
    for ch, p in units:
        lf, rt = pc(left[ch], p), pc(right[ch], p)
        grams[ch, p, 0] = _dot_nt(jnp.where(lo, lf, 0.0), rt)
        grams[ch, p, 1] = _dot_nt(jnp.where(lo, 0.0, lf), rt)
    keys = [(ch, p, h) for ch, p in units for h in range(2)]
    invs = dict(zip(keys, _tri_inverses(
        [jnp.where(cc < rr, grams[k][:t, :t], 0.0) for k in keys], t)))

    x_keys, lower = {}, {}
    for ch, p in units:
        vv = jnp.concatenate([pc(vs[ch], p)] * 2, axis=0)
        x_keys[ch, p] = jnp.where(lo, _dot(jnp.where(strict_k, grams[ch, p, 0][:t], 0.0), vv),
                                  _dot(jnp.where(strict_k, grams[ch, p, 1][:t], 0.0), vv))
        for h in range(2):
            lower[ch, p, h] = jnp.where(incl2, grams[ch, p, h][t:], 0.0).astype(BF16)

    state = [st_ref[p] for p in range(n_pairs)]
    ys = []
    for ch in range(n_ch):
        y_ch = []
        for p in range(n_pairs):
            from_state = _dot_nt(pc(left[ch], p), state[p])
            x = from_state[:t] + x_keys[ch, p]
            u = jnp.where(lo, _dot(invs[ch, p, 0], x), _dot(invs[ch, p, 1], x))
            uv = jnp.concatenate([u, pc(vs[ch], p)], axis=0)
            y_ch.append(from_state[t:] + jnp.where(lo, _dot(lower[ch, p, 0], uv),
                                                   _dot(lower[ch, p, 1], uv)))
            state[p] = jnp.where(same_head, state[p] * pc(decay_end[ch], p)
                                 + _dot_tn(uv, pc(right_end[ch], p)), 0.0)
        ys.append(jnp.concatenate(y_ch, axis=1))
    for p in range(n_pairs):
        st_ref[p] = state[p]
    return jnp.concatenate(ys, axis=0)


def _rwkv_kernel(r_ref, k_ref, v_ref, l_ref, mur_ref, muk_ref, muv_ref, mul_ref,
                 w0_ref, w2_ref, a0_ref, a2_ref, g2_ref, kk_ref, ka_ref, rk_ref,
                 lnw_ref, lnb_ref, o_ref, st_ref, pr_ref, pk_ref, pv_ref, pl_ref, *, tc, t):
    @pl.when(pl.program_id(1) == 0)
    def _():
        st_ref[...] = jnp.zeros_like(st_ref)
        pr_ref[...] = jnp.zeros_like(pr_ref)
        pk_ref[...] = jnp.zeros_like(pk_ref)
        pv_ref[...] = jnp.zeros_like(pv_ref)
        pl_ref[...] = jnp.zeros_like(pl_ref)

    row = lax.broadcasted_iota(jnp.int32, (tc, 1), 0)

    def shifted(x_ref, p_ref, mu_ref):
        x = x_ref[...]
        prev = jnp.where(row == 0, p_ref[...], pltpu.roll(x, 1, axis=0))
        p_ref[...] = x[tc - 1:tc, :]
        return x + (prev - x) * mu_ref[...]

    r = shifted(r_ref, pr_ref, mur_ref)
    k = shifted(k_ref, pk_ref, muk_ref)
    v = shifted(v_ref, pv_ref, muv_ref)
    lora = shifted(l_ref, pl_ref, mul_ref)
    wa = lora[:, :LANES]
    gl = lora[:, LANES:]
    w_raw = w0_ref[...] + _dot(jnp.tanh(wa), w2_ref[...])
    lw = -math.exp(-0.5) * jax.nn.sigmoid(w_raw)
    a = jax.nn.sigmoid(a0_ref[...] + _dot(wa, a2_ref[...]))
    g = _dot(jax.nn.sigmoid(gl), g2_ref[...])
    kk = k * kk_ref[...]
    k2 = k * (1.0 + (a - 1.0) * ka_ref[...])

    lo = _lo_mask()
    kkn = kk / jnp.maximum(jnp.sqrt(_headsum(kk * kk, lo)), 1e-12)
    bonus = _headsum(r * k2 * rk_ref[...], lo) * v
    y = _rwkv_tile(r, k2, v, kkn, a, lw, st_ref, lo, t)
    mean = _headsum(y, lo) * (1.0 / HEAD_DIM)
    d = y - mean
    var = _headsum(d * d, lo) * (1.0 / HEAD_DIM)
    yn = d * lax.rsqrt(var + RWKV_GN_EPS) * lnw_ref[...] + lnb_ref[...]
    o_ref[...] = (yn + bonus) * g


def _rwkv(z, prm, *, col_r, col_l, width, lora_w, tc):
    batch, seq, _ = z.shape
    n_pairs = width // PAIR
    blk = lambda c: pl.BlockSpec((None, tc, width), lambda b, i: (b, i, c))
    vec = lambda w: pl.BlockSpec((1, w), lambda b, i: (0, 0))
    mat = lambda s: pl.BlockSpec(s, lambda b, i: (0, 0))
    in_specs = [blk(col_r), blk(col_r + 1), blk(col_r + 2),
                pl.BlockSpec((None, tc, lora_w), lambda b, i: (b, i, col_l)),
                vec(width), vec(width), vec(width), vec(lora_w),
                vec(width), mat((LANES, width)), vec(width), mat((LANES, width)),
                mat((lora_w - LANES, width)),
                vec(width), vec(width), vec(width), vec(width), vec(width)]
    return pl.pallas_call(
        functools.partial(_rwkv_kernel, tc=tc, t=RWKV_CHUNK),
        grid=(batch, seq // tc),
        in_specs=in_specs,
        out_specs=pl.BlockSpec((None, tc, width), lambda b, i: (b, i, 0)),
        out_shape=jax.ShapeDtypeStruct((batch, seq, width), F32),
        scratch_shapes=[pltpu.VMEM((n_pairs, PAIR, PAIR), F32),
                        pltpu.VMEM((1, width), F32), pltpu.VMEM((1, width), F32),
                        pltpu.VMEM((1, width), F32), pltpu.VMEM((1, lora_w), F32)],
        compiler_params=_params(("parallel", "arbitrary")),
        name="rwkv7",
    )(z, z, z, z, *prm)


def _xattn_kernel(x_ref, kv_ref, g_ref, wq_ref, wo_ref, qg_ref, kg_ref, o_ref, kn_ref, vb_ref):
    width = XATTN_HEADS * XATTN_HEAD_DIM

    @pl.when(pl.program_id(1) == 0)
    def _():
        for h in range(XATTN_HEADS):
            cols = slice(h * XATTN_HEAD_DIM, (h + 1) * XATTN_HEAD_DIM)
            kn_ref[:, cols] = _rms(kv_ref[:, cols], kg_ref[...]).astype(BF16)
        vb_ref[...] = kv_ref[:, width:].astype(BF16)

    x = x_ref[...]
    q = _dot(_rms(x, g_ref[...]), wq_ref[...])
    scale = XATTN_HEAD_DIM ** -0.5
    outs = []
    for h in range(XATTN_HEADS):
        cols = slice(h * XATTN_HEAD_DIM, (h + 1) * XATTN_HEAD_DIM)
        qh = _rms(q[:, cols], qg_ref[...])
        s = _dot_nt(qh, kn_ref[:, cols]) * scale
        m = jnp.max(s, axis=-1, keepdims=True)
        p = jnp.exp(s - m)
        l = jnp.sum(p, axis=-1, keepdims=True)
        outs.append(_dot(p / l, vb_ref[:, cols]))
    o_ref[...] = x + _dot(jnp.concatenate(outs, axis=-1), wo_ref[...])


def _xattn(x, kv, gain, w_q, w_o, q_gain, k_gain, *, tq):
    batch, seq, d = x.shape
    mem_len = kv.shape[1]
    width = XATTN_HEADS * XATTN_HEAD_DIM
    return pl.pallas_call(
        _xattn_kernel,
        grid=(batch, seq // tq),
        in_specs=[pl.BlockSpec((None, tq, d), lambda b, i: (b, i, 0)),
                  pl.BlockSpec((None, mem_len, 2 * width), lambda b, i: (b, 0, 0)),
                  _resident((1, d)), _resident((d, width)), _resident((width, d)),
                  _resident((1, XATTN_HEAD_DIM)), _resident((1, XATTN_HEAD_DIM))],
        out_specs=pl.BlockSpec((None, tq, d), lambda b, i: (b, i, 0)),
        out_shape=jax.ShapeDtypeStruct((batch, seq, d), F32),
        scratch_shapes=[pltpu.VMEM((mem_len, width), BF16), pltpu.VMEM((mem_len, width), BF16)],
        compiler_params=_params(("parallel", "arbitrary")),
        name="xattn",
    )(x, kv, gain.reshape(1, d), w_q.astype(BF16), w_o.astype(BF16),
      q_gain.reshape(1, -1), k_gain.reshape(1, -1))


def _rope_tables(positions):
    inv_freq = ROPE_THETA ** (-jnp.arange(0, ROPE_DIM, 2, dtype=F32) / ROPE_DIM)
    ang = positions.astype(F32)[..., None] * inv_freq
    cos, sin = jnp.cos(ang), jnp.sin(ang)
    rest = HEAD_DIM - ROPE_DIM
    ones = jnp.ones(cos.shape[:-1] + (rest,), F32)
    zeros = jnp.zeros(cos.shape[:-1] + (rest,), F32)
    cos_f = jnp.concatenate([cos, cos, ones], axis=-1)
    sin_f = jnp.concatenate([-sin, sin, zeros], axis=-1)
    return jnp.tile(cos_f, 2), jnp.tile(sin_f, 2)


def _pad_rows(w, rows, at):
    out = jnp.zeros((rows, w.shape[1]), w.dtype)
    return lax.dynamic_update_slice(out, w, (at, 0))


def _mixers(x, cos_f, sin_f, mix_norm, mix_w_in, fox_f_bias, fox_q_gain, fox_k_gain,
            moba_q_gain, moba_k_gain, rwkv_mu, rwkv_w0, rwkv_w2, rwkv_a0, rwkv_a2, rwkv_g2,
            rwkv_k_k, rwkv_k_a, rwkv_r_k, rwkv_ln_w, rwkv_ln_b, *, tm):
    batch, seq, d = x.shape
    n = batch * seq
    fox_heads = fox_f_bias.shape[0]
    fox_w = fox_heads * HEAD_DIM
    moba_w = fox_w
    rwkv_w = rwkv_w0.shape[0]
    dec_l, icl_l, gate_l = rwkv_w2.shape[0], rwkv_a2.shape[0], rwkv_g2.shape[0]
    assert fox_w % PAIR == 0 and rwkv_w % PAIR == 0 and dec_l + icl_l == LANES
    fox_in = 3 * fox_w + fox_heads
    moba_in = 3 * moba_w
    lora_in = dec_l + icl_l + gate_l
    lora_w = LANES * (-(-(lora_in + fox_heads) // LANES))
    in_width = mix_w_in.shape[1]
    head = 3 * fox_w + moba_in + 3 * rwkv_w
    assert head % lora_w == 0 and head % rwkv_w == 0
    r0 = fox_in + moba_in
    w_perm = jnp.concatenate([
        mix_w_in[:, :3 * fox_w], mix_w_in[:, fox_in:fox_in + moba_in],
        mix_w_in[:, r0:r0 + 3 * rwkv_w], mix_w_in[:, r0 + 3 * rwkv_w:in_width],
        mix_w_in[:, 3 * fox_w:fox_in],
        jnp.zeros((d, lora_w - lora_in - fox_heads), F32)], axis=1)
    total = w_perm.shape[1]
    wf_t = _pad_rows(mix_w_in[:, 3 * fox_w:fox_in].T, 8, 0)

    x2 = x.reshape(n, d)
    tn = total // 3 if total % (3 * LANES) == 0 else LANES
    z2, f_t = _norm_proj(x2, mix_norm, w_perm, tm=tm, tn=tn, wt=wf_t)
    z = z2.reshape(batch, seq, total)

    bias8 = _pad_rows(fox_f_bias.reshape(fox_heads, 1), 8, 0)
    c = _fox_c(f_t, bias8, batch=batch, seq=seq)
    c4 = c[:, :fox_heads].reshape(batch, fox_heads // 2, 2, seq)
    y_fox = _fox_attention(z, c4, fox_q_gain, fox_k_gain, tq=256)
    y_moba = _moba_attention(z, cos_f, sin_f, moba_q_gain, moba_k_gain,
                             col0=3 * fox_w // PAIR, n_pairs=moba_w // PAIR)

    mu_l = jnp.concatenate([rwkv_mu[3 * rwkv_w:], jnp.zeros((lora_w - lora_in,), F32)])
    prm = [rwkv_mu[:rwkv_w], rwkv_mu[rwkv_w:2 * rwkv_w], rwkv_mu[2 * rwkv_w:3 * rwkv_w], mu_l,
           rwkv_w0, _pad_rows(rwkv_w2, LANES, 0).astype(BF16),
           rwkv_a0, _pad_rows(rwkv_a2, LANES, dec_l).astype(BF16),
           _pad_rows(rwkv_g2, lora_w - LANES, 0).astype(BF16),
           rwkv_k_k, rwkv_k_a, rwkv_r_k.reshape(-1), rwkv_ln_w, rwkv_ln_b]
    prm = [p.reshape(1, -1) if p.ndim == 1 else p for p in prm]
    y_rwkv = _rwkv(z, prm, col_r=(3 * fox_w + moba_in) // rwkv_w, col_l=head // lora_w,
                   width=rwkv_w, lora_w=lora_w, tc=256)

    return y_fox, y_rwkv, y_moba


def _mixer_layer(x, cos_f, sin_f, mix_norm, mix_w_in, mix_w_out, *mixer_params, tm):
    batch, seq, d = x.shape
    n = batch * seq
    ys = _mixers(x, cos_f, sin_f, mix_norm, mix_w_in, *mixer_params, tm=tm)
    acts = [y.reshape(n, y.shape[-1]) for y in ys]
    bounds = [0]
    for a in acts:
        bounds.append(bounds[-1] + a.shape[1])
    ws = [mix_w_out[lo:hi] for lo, hi in zip(bounds[:-1], bounds[1:])]
    return _out_proj(x.reshape(n, d), acts, ws, tm=tm).reshape(batch, seq, d)


def kernel(x, mem, positions, ffn1_norm, ffn1_w_in, ffn1_w_out, mix_norm, mix_w_in, mix_w_out, fox_f_bias, fox_q_gain, fox_k_gain, moba_q_gain, moba_k_gain, rwkv_mu, rwkv_w0, rwkv_w2, rwkv_a0, rwkv_a2, rwkv_g2, rwkv_k_k, rwkv_k_a, rwkv_r_k, rwkv_ln_w, rwkv_ln_b, xattn_norm, xattn_mem_norm, xattn_w_q, xattn_w_kv, xattn_q_gain, xattn_k_gain, xattn_w_out, ffn2_norm, ffn2_w_in, ffn2_w_out):
    batch, seq, d = x.shape
    n = batch * seq
    depth = ffn1_norm.shape[0]
    tm = 512 if n % 512 == 0 else 256
    tf = 256
    cos_f, sin_f = _rope_tables(positions)
    mem2 = mem.reshape(-1, d)
    for l in range(depth):
        x = _ffn(x.reshape(n, d), ffn1_norm[l], ffn1_w_in[l], ffn1_w_out[l], tm=tm, tf=tf)
        x = x.reshape(batch, seq, d)
        x = _mixer_layer(x, cos_f, sin_f, mix_norm[l], mix_w_in[l], mix_w_out[l], fox_f_bias[l],
                         fox_q_gain[l], fox_k_gain[l], moba_q_gain[l], moba_k_gain[l], rwkv_mu[l],
                         rwkv_w0[l], rwkv_w2[l], rwkv_a0[l], rwkv_a2[l], rwkv_g2[l], rwkv_k_k[l],
                         rwkv_k_a[l], rwkv_r_k[l], rwkv_ln_w[l], rwkv_ln_b[l], tm=tm)
        kv = _norm_proj(mem2, xattn_mem_norm[l], xattn_w_kv[l], tm=256, tn=512)
        x = _xattn(x, kv.reshape(batch, -1, kv.shape[-1]), xattn_norm[l], xattn_w_q[l],
                   xattn_w_out[l], xattn_q_gain[l], xattn_k_gain[l], tq=512 if seq % 512 == 0 else 256)
        x = _ffn(x.reshape(n, d), ffn2_norm[l], ffn2_w_in[l], ffn2_w_out[l], tm=tm, tf=tf)
        x = x.reshape(batch, seq, d)
    return x
```

```python
import functools
import math

import jax
import jax.numpy as jnp
from jax import lax
from jax.experimental import pallas as pl
from jax.experimental.pallas import tpu as pltpu

F32 = jnp.float32
BF16 = jnp.bfloat16
HIGHEST = lax.Precision.HIGHEST

HEAD_DIM = 64
PAIR = 2 * HEAD_DIM
LANES = 128
NORM_EPS = 1e-6
RWKV_GN_EPS = 64e-5
MOBA_BLOCK = 256
MOBA_TOPK = 3
ROPE_THETA = 500000.0
ROPE_DIM = HEAD_DIM // 4
ROPE_HALF = ROPE_DIM // 2
XATTN_HEADS = 4
XATTN_HEAD_DIM = 128
RWKV_CHUNK = 64
MASKED = -1e30
KEY_SPAN = 1024
LOG2E = math.log2(math.e)
VMEM_LIMIT = 56 * 1024 * 1024


def _params(sem):
    return pltpu.CompilerParams(dimension_semantics=sem, vmem_limit_bytes=VMEM_LIMIT)


def _dot(a, b):
    return jnp.dot(a.astype(BF16), b.astype(BF16), preferred_element_type=F32)


def _dot_nt(a, b):
    return lax.dot_general(a.astype(BF16), b.astype(BF16), (((1,), (1,)), ((), ())),
                           preferred_element_type=F32)


def _dot_tn(a, b):
    return lax.dot_general(a.astype(BF16), b.astype(BF16), (((0,), (0,)), ((), ())),
                           preferred_element_type=F32)


def _resident(shape):
    nd = len(shape)
    return pl.BlockSpec(shape, lambda *_: (0,) * nd, pipeline_mode=pl.Buffered(1))


def _rms(x, gain):
    ms = jnp.mean(x * x, axis=-1, keepdims=True)
    return x * lax.rsqrt(ms + NORM_EPS) * gain


def _lo_mask():
    return lax.broadcasted_iota(jnp.int32, (1, PAIR), 1) < HEAD_DIM


def _halfsum(x, lo):
    s0 = jnp.sum(jnp.where(lo, x, 0.0), axis=-1, keepdims=True)
    s1 = jnp.sum(jnp.where(lo, 0.0, x), axis=-1, keepdims=True)
    return jnp.where(lo, s0, s1)


def _pair_rms(x, gain, lo):
    ms = _halfsum(x * x, lo) * (1.0 / HEAD_DIM)
    return x * lax.rsqrt(ms + NORM_EPS) * gain


def _ffn_kernel(x_ref, g_ref, wg_ref, wu_ref, wo_ref, o_ref, h_ref, acc_ref, *, nf):
    h_ref[...] = _rms(x_ref[...], g_ref[...]).astype(BF16)
    acc_ref[...] = jnp.zeros_like(acc_ref)

    def body(f, carry):
        h = h_ref[...]
        gate = jnp.dot(h, wg_ref[f], preferred_element_type=F32)
        up = jnp.dot(h, wu_ref[f], preferred_element_type=F32)
        act = (gate * jax.nn.sigmoid(gate) * up).astype(BF16)
        acc_ref[...] += jnp.dot(act, wo_ref[f], preferred_element_type=F32)
        return carry

    lax.fori_loop(0, nf, body, 0)
    o_ref[...] = x_ref[...] + 0.5 * acc_ref[...]


def _ffn(x2, gain, w_in, w_out, *, tm, tf):
    n, d = x2.shape
    ffn = w_out.shape[0]
    nf = ffn // tf
    wg = w_in[:, :ffn].astype(BF16).reshape(d, nf, tf).transpose(1, 0, 2)
    wu = w_in[:, ffn:].astype(BF16).reshape(d, nf, tf).transpose(1, 0, 2)
    wo = w_out.astype(BF16).reshape(nf, tf, d)
    return pl.pallas_call(
        functools.partial(_ffn_kernel, nf=nf),
        grid=(n // tm,),
        in_specs=[pl.BlockSpec((tm, d), lambda i: (i, 0)),
                  _resident((1, d)), _resident((nf, d, tf)), _resident((nf, d, tf)),
                  _resident((nf, tf, d))],
        out_specs=pl.BlockSpec((tm, d), lambda i: (i, 0)),
        out_shape=jax.ShapeDtypeStruct((n, d), F32),
        scratch_shapes=[pltpu.VMEM((tm, d), BF16), pltpu.VMEM((tm, d), F32)],
        compiler_params=_params(("parallel",)),
        name="ffn",
    )(x2, gain.reshape(1, d), wg, wu, wo)


def _proj_kernel(x_ref, g_ref, w_ref, o_ref, h_ref):
    @pl.when(pl.program_id(1) == 0)
    def _():
        h_ref[...] = _rms(x_ref[...], g_ref[...]).astype(BF16)

    o_ref[...] = jnp.dot(h_ref[...], w_ref[...], preferred_element_type=F32)


def _proj_t_kernel(x_ref, g_ref, w_ref, wt_ref, o_ref, ot_ref, *, tn):
    h = _rms(x_ref[...], g_ref[...]).astype(BF16)
    ot_ref[...] = _dot_nt(wt_ref[...], h)
    for c in range(0, w_ref.shape[1], tn):
        o_ref[:, c:c + tn] = jnp.dot(h, w_ref[:, c:c + tn], preferred_element_type=F32)


def _norm_proj_wide(x2, gain, w, wt, *, tm, tn):
    n, d = x2.shape
    nc = w.shape[1]
    rows = wt.shape[0]
    return pl.pallas_call(
        functools.partial(_proj_t_kernel, tn=tn),
        grid=(n // tm,),
        in_specs=[pl.BlockSpec((tm, d), lambda i: (i, 0)), _resident((1, d)),
                  _resident((d, nc)), _resident((rows, d))],
        out_specs=[pl.BlockSpec((tm, nc), lambda i: (i, 0)),
                   pl.BlockSpec((rows, tm), lambda i: (0, i))],
        out_shape=[jax.ShapeDtypeStruct((n, nc), F32), jax.ShapeDtypeStruct((rows, n), F32)],
        compiler_params=_params(("parallel",)),
        name="mix_proj",
    )(x2, gain.reshape(1, d), w.astype(BF16), wt.astype(BF16))


def _norm_proj(x2, gain, w, *, tm, tn):
    n, d = x2.shape
    nc = w.shape[1]
    in_specs = [pl.BlockSpec((tm, d), lambda i, j: (i, 0)),
                pl.BlockSpec((1, d), lambda i, j: (0, 0)),
                pl.BlockSpec((d, tn), lambda i, j: (0, j))]
    out_specs = pl.BlockSpec((tm, tn), lambda i, j: (i, j))
    out_shape = jax.ShapeDtypeStruct((n, nc), F32)
    return pl.pallas_call(
        _proj_kernel,
        grid=(n // tm, nc // tn),
        in_specs=in_specs, out_specs=out_specs, out_shape=out_shape,
        scratch_shapes=[pltpu.VMEM((tm, d), BF16)],
        compiler_params=_params(("parallel", "arbitrary")),
        name="norm_proj",
    )(x2, gain.reshape(1, d), w.astype(BF16))


def _outproj_kernel(*refs, n_in):
    x_ref = refs[0]
    a_refs = refs[1:1 + n_in]
    w_refs = refs[1 + n_in:1 + 2 * n_in]
    o_ref = refs[1 + 2 * n_in]
    acc = x_ref[...]
    for a_ref, w_ref in zip(a_refs, w_refs):
        acc = acc + _dot(a_ref[...], w_ref[...])
    o_ref[...] = acc


def _out_proj(x2, acts, ws, *, tm):
    n, d = x2.shape
    n_in = len(acts)
    in_specs = [pl.BlockSpec((tm, d), lambda i: (i, 0))]
    in_specs += [pl.BlockSpec((tm, a.shape[1]), lambda i: (i, 0)) for a in acts]
    in_specs += [_resident(w.shape) for w in ws]
    return pl.pallas_call(
        functools.partial(_outproj_kernel, n_in=n_in),
        grid=(n // tm,),
        in_specs=in_specs,
        out_specs=pl.BlockSpec((tm, d), lambda i: (i, 0)),
        out_shape=jax.ShapeDtypeStruct((n, d), F32),
        compiler_params=_params(("parallel",)),
        name="out_proj",
    )(x2, *acts, *[w.astype(BF16) for w in ws])


def _fox_c_kernel(f_ref, b_ref, c_ref, *, seq):
    z = f_ref[...] + b_ref[...]
    logf = jnp.minimum(z, 0.0) - jnp.log1p(jnp.exp(-jnp.abs(z)))
    r = lax.broadcasted_iota(jnp.int32, (LANES, LANES), 0)
    c = lax.broadcasted_iota(jnp.int32, (LANES, LANES), 1)
    upper = (r <= c).astype(F32)
    carry = jnp.zeros((f_ref.shape[0], 1), F32)
    for ch in range(seq // LANES):
        seg = logf[:, ch * LANES:(ch + 1) * LANES]
        cs = jnp.dot(seg, upper, precision=HIGHEST, preferred_element_type=F32) + carry
        c_ref[:, ch * LANES:(ch + 1) * LANES] = cs
        carry = cs[:, LANES - 1:LANES]


def _fox_c(f_t, bias8, *, batch, seq):
    rows = f_t.shape[0]
    return pl.pallas_call(
        functools.partial(_fox_c_kernel, seq=seq),
        grid=(batch,),
        in_specs=[pl.BlockSpec((rows, seq), lambda b: (0, b)),
                  pl.BlockSpec((rows, 1), lambda b: (0, 0))],
        out_specs=pl.BlockSpec((None, rows, seq), lambda b: (b, 0, 0)),
        out_shape=jax.ShapeDtypeStruct((batch, rows, seq), F32),
        compiler_params=_params(("parallel",)),
        name="fox_c",
    )(f_t, bias8)


def _softmax_block(s, m, acc, v):
    m_new = jnp.maximum(m, jnp.max(s, axis=-1, keepdims=True))
    alpha = jnp.exp2(m - m_new)
    return m_new, alpha * acc + _dot(jnp.exp2(s - m_new), v)


def _softmax_first(s, v):
    m = jnp.max(s, axis=-1, keepdims=True)
    return m, _dot(jnp.exp2(s - m), v)


def _softmax_finish(acc0, acc1, lo):
    return jnp.where(lo, acc0 / pltpu.roll(acc0, HEAD_DIM, axis=1),
                     acc1 / pltpu.roll(acc1, HEAD_DIM, axis=1))


def _store_values(vb_ref, rows, v, lo):
    vb_ref[0, rows, :] = jnp.where(lo, v, 1.0).astype(BF16)
    vb_ref[1, rows, :] = jnp.where(lo, 1.0, v).astype(BF16)


def _fox_kernel(q_ref, k_ref, v_ref, c_ref, qg_ref, kg_ref, o_ref, kn_ref, vb_ref, *, tq, seq, tks):
    i = pl.program_id(2)
    lo = _lo_mask()

    @pl.when(i == 0)
    def _():
        def prep(j, carry):
            rows = pl.ds(pl.multiple_of(j * tq, tq), tq)
            kn_ref[rows, :] = _pair_rms(k_ref[rows, :], kg_ref[...], lo).astype(BF16)
            _store_values(vb_ref, rows, v_ref[rows, :], lo)
            return carry
        lax.fori_loop(0, seq // tq, prep, 0)

    q = _pair_rms(q_ref[...], qg_ref[...], lo)
    qh = (jnp.where(lo, q, 0.0).astype(BF16), jnp.where(lo, 0.0, q).astype(BF16))

    start = pl.multiple_of(i * tq, tq)
    c_first = c_ref[:, pl.ds(start, tq)][:, 0:1]
    n_full = (i * tq) // tks
    rem = pl.multiple_of(n_full * tks, tks)

    k_r = kn_ref[pl.ds(rem, tks), :]
    q_pos = start + lax.broadcasted_iota(jnp.int32, (tq, 1), 0)
    k_pos = rem + lax.broadcasted_iota(jnp.int32, (1, tks), 1)
    visible = k_pos <= q_pos
    bias_r = (c_first - c_ref[:, pl.ds(rem, tks)]) * LOG2E
    state = []
    for h in range(2):
        s = _dot_nt(qh[h], k_r) + bias_r[h:h + 1, :]
        state.extend(_softmax_first(jnp.where(visible, s, MASKED), vb_ref[h, pl.ds(rem, tks), :]))

    def body(j, st):
        off = pl.multiple_of(j * tks, tks)
        kj = kn_ref[pl.ds(off, tks), :]
        bias = (c_first - c_ref[:, pl.ds(off, tks)]) * LOG2E
        out = []
        for h in range(2):
            s = _dot_nt(qh[h], kj) + bias[h:h + 1, :]
            out.extend(_softmax_block(s, st[2 * h], st[2 * h + 1], vb_ref[h, pl.ds(off, tks), :]))
        return tuple(out)

    _, a0, _, a1 = lax.fori_loop(0, n_full, body, tuple(state))
    o_ref[...] = _softmax_finish(a0, a1, lo)


def _fox_attention(z, c4, q_gain, k_gain, *, tq):
    batch, seq, _ = z.shape
    n_pairs = c4.shape[1]
    scale = HEAD_DIM ** -0.5 * LOG2E
    qg = jnp.tile(q_gain, 2).reshape(1, PAIR) * scale
    kg = jnp.tile(k_gain, 2).reshape(1, PAIR)
    tks = min(KEY_SPAN, seq)
    assert seq % tks == 0 and tks % tq == 0
    return pl.pallas_call(
        functools.partial(_fox_kernel, tq=tq, seq=seq, tks=tks),
        grid=(batch, n_pairs, seq // tq),
        in_specs=[pl.BlockSpec((None, tq, PAIR), lambda b, p, i: (b, i, p)),
                  pl.BlockSpec((None, seq, PAIR), lambda b, p, i: (b, 0, n_pairs + p)),
                  pl.BlockSpec((None, seq, PAIR), lambda b, p, i: (b, 0, 2 * n_pairs + p)),
                  pl.BlockSpec((None, None, 2, seq), lambda b, p, i: (b, p, 0, 0)),
                  pl.BlockSpec((1, PAIR), lambda b, p, i: (0, 0)),
                  pl.BlockSpec((1, PAIR), lambda b, p, i: (0, 0))],
        out_specs=pl.BlockSpec((None, tq, PAIR), lambda b, p, i: (b, i, p)),
        out_shape=jax.ShapeDtypeStruct((batch, seq, n_pairs * PAIR), F32),
        scratch_shapes=[pltpu.VMEM((seq, PAIR), BF16), pltpu.VMEM((2, seq, PAIR), BF16)],
        compiler_params=_params(("parallel", "parallel", "arbitrary")),
        name="fox_attn",
    )(z, z, z, c4, qg, kg)


def _rope(x, cos_f, sin_f):
    lane = lax.broadcasted_iota(jnp.int32, (1, PAIR), 1) % HEAD_DIM
    partner = jnp.where(lane < ROPE_HALF,
                        pltpu.roll(x, PAIR - ROPE_HALF, axis=1),
                        pltpu.roll(x, ROPE_HALF, axis=1))
    return x * cos_f + partner * sin_f


def _topk_mask(gate, n_valid):
    lane = lax.broadcasted_iota(jnp.int32, gate.shape, 1)
    lane_f = lane.astype(F32)
    neg = -jnp.inf
    g = jnp.where(lane < n_valid, gate, neg)
    sel = jnp.zeros(gate.shape, F32)
    for _ in range(MOBA_TOPK):
        mx = jnp.max(g, axis=-1, keepdims=True)
        first = jnp.min(jnp.where(g == mx, lane_f, float(LANES)), axis=-1, keepdims=True)
        pick = jnp.logical_and(lane_f == first, mx > neg)
        sel = jnp.where(pick, 1.0, sel)
        g = jnp.where(pick, neg, g)
    return sel


def _moba_kernel(q_ref, k_ref, v_ref, cq_ref, sq_ref, ck_ref, sk_ref, qg_ref, kg_ref,
                 o_ref, kn_ref, vb_ref, km_ref, *, tq, tb, seq, tks):
    i = pl.program_id(2)
    lo = _lo_mask()
    n_blk = seq // tb

    @pl.when(i == 0)
    def _():
        km_ref[...] = jnp.zeros_like(km_ref)
        for n in range(n_blk):
            rows = pl.ds(n * tb, tb)
            kn = _rope(_pair_rms(k_ref[rows, :], kg_ref[...], lo), ck_ref[rows, :], sk_ref[rows, :])
            kn_ref[rows, :] = kn.astype(BF16)
            _store_values(vb_ref, rows, v_ref[rows, :], lo)
            km_ref[n:n + 1, :] = jnp.mean(kn, axis=0, keepdims=True)

    q = _rope(_pair_rms(q_ref[...], qg_ref[...], lo), cq_ref[...], sq_ref[...])
    q_lo = jnp.where(lo, q, 0.0)
    q_hi = jnp.where(lo, 0.0, q)
    km = km_ref[...]
    scale = HEAD_DIM ** -0.5 * LOG2E
    rr = lax.broadcasted_iota(jnp.int32, (tq, 1), 0)
    sub = jnp.zeros((tq, 1), jnp.int32)
    for edge in range(tb, tq, tb):
        sub = sub + (rr >= edge).astype(jnp.int32)
    q_blk = i * (tq // tb) + sub
    row_in_blk = rr - sub * tb

    qh, sel = [], []
    for qm in (q_lo, q_hi):
        gate = lax.dot_general(qm, km, (((1,), (1,)), ((), ())), precision=HIGHEST,
                               preferred_element_type=F32)
        sel.append(_topk_mask(gate, q_blk))
        qh.append((qm * scale).astype(BF16))

    per_span = tks // tb
    n_full = (i * tq) // tks
    lane = lax.broadcasted_iota(jnp.int32, (1, LANES), 1)
    cc = lax.broadcasted_iota(jnp.int32, (1, tb), 1)

    def masked_scores(h, span, k_span, with_own):
        s = _dot_nt(qh[h], k_span)
        parts = []
        for b in range(per_span):
            blk = span * per_span + b
            chosen = jnp.max(jnp.where(lane == blk, sel[h], 0.0), axis=-1, keepdims=True)
            keep = chosen > 0.0
            if with_own:
                last_visible = jnp.where(q_blk == blk, row_in_blk, -1)
                keep = jnp.logical_or(keep, cc <= last_visible)
            parts.append(jnp.where(keep, s[:, b * tb:(b + 1) * tb], MASKED))
        return jnp.concatenate(parts, axis=1)

    rem = pl.multiple_of(n_full * tks, tks)
    k_r = kn_ref[pl.ds(rem, tks), :]
    state = []
    for h in range(2):
        state.extend(_softmax_first(masked_scores(h, n_full, k_r, True),
                                    vb_ref[h, pl.ds(rem, tks), :]))

    def body(j, st):
        off = pl.multiple_of(j * tks, tks)
        kj = kn_ref[pl.ds(off, tks), :]
        out = []
        for h in range(2):
            s = masked_scores(h, j, kj, False)
            out.extend(_softmax_block(s, st[2 * h], st[2 * h + 1], vb_ref[h, pl.ds(off, tks), :]))
        return tuple(out)

    _, a0, _, a1 = lax.fori_loop(0, n_full, body, tuple(state))
    o_ref[...] = _softmax_finish(a0, a1, lo)


def _moba_attention(z, cos_f, sin_f, q_gain, k_gain, *, col0, n_pairs):
    batch, seq, _ = z.shape
    tb = MOBA_BLOCK
    tq = 2 * tb if seq % (2 * tb) == 0 else tb
    assert seq % tq == 0 and seq // tb <= LANES
    qg = jnp.tile(q_gain, 2).reshape(1, PAIR)
    kg = jnp.tile(k_gain, 2).reshape(1, PAIR)
    tile = pl.BlockSpec((None, tq, PAIR), lambda b, p, i: (b, i, 0))
    full = pl.BlockSpec((None, seq, PAIR), lambda b, p, i: (b, 0, 0))
    tks = min(KEY_SPAN, seq)
    assert seq % tks == 0 and tks % tq == 0
    return pl.pallas_call(
        functools.partial(_moba_kernel, tq=tq, tb=tb, seq=seq, tks=tks),
        grid=(batch, n_pairs, seq // tq),
        in_specs=[pl.BlockSpec((None, tq, PAIR), lambda b, p, i: (b, i, col0 + p)),
                  pl.BlockSpec((None, seq, PAIR), lambda b, p, i: (b, 0, col0 + n_pairs + p)),
                  pl.BlockSpec((None, seq, PAIR), lambda b, p, i: (b, 0, col0 + 2 * n_pairs + p)),
                  tile, tile, full, full,
                  pl.BlockSpec((1, PAIR), lambda b, p, i: (0, 0)),
                  pl.BlockSpec((1, PAIR), lambda b, p, i: (0, 0))],
        out_specs=pl.BlockSpec((None, tq, PAIR), lambda b, p, i: (b, i, p)),
        out_shape=jax.ShapeDtypeStruct((batch, seq, n_pairs * PAIR), F32),
        scratch_shapes=[pltpu.VMEM((seq, PAIR), BF16), pltpu.VMEM((2, seq, PAIR), BF16),
                        pltpu.VMEM((LANES, PAIR), F32)],
        compiler_params=_params(("parallel", "parallel", "arbitrary")),
        name="moba_attn",
    )(z, z, z, cos_f, sin_f, cos_f, sin_f, qg, kg)


def _tri_inverses(n_mats, t):
    r = lax.broadcasted_iota(jnp.int32, (t, t), 0)
    c = lax.broadcasted_iota(jnp.int32, (t, t), 1)
    eye = (r == c).astype(F32)
    ms = [eye + n for n in n_mats]
    ps = list(n_mats)
    for _ in range(int(math.log2(t)) - 1):
        ps = [_dot(p, p) for p in ps]
        ms = [m + _dot(m, p) for m, p in zip(ms, ps)]
    return ms


def _headsum(x, lo):
    return jnp.concatenate([_halfsum(x[:, c:c + PAIR], lo) for c in range(0, x.shape[1], PAIR)],
                           axis=1)


def _rwkv_tile(r, k2, v, kkn, a, lw, st_ref, lo, t):
    tc, width = r.shape
    n_ch, n_pairs = tc // t, width // PAIR
    rr = lax.broadcasted_iota(jnp.int32, (t, t), 0)
    cc = lax.broadcasted_iota(jnp.int32, (t, t), 1)
    tril = (cc <= rr).astype(F32)
    r2 = lax.broadcasted_iota(jnp.int32, (t, 2 * t), 0)
    c2 = lax.broadcasted_iota(jnp.int32, (t, 2 * t), 1)
    strict_k = jnp.logical_and(c2 >= t, c2 - t < r2)
    incl2 = jnp.where(c2 >= t, c2 - t, c2) <= r2
    same_head = (lax.broadcasted_iota(jnp.int32, (PAIR, PAIR), 0) < HEAD_DIM) == (
        lax.broadcasted_iota(jnp.int32, (PAIR, PAIR), 1) < HEAD_DIM)
    beta = kkn * a
    units = [(ch, p) for ch in range(n_ch) for p in range(n_pairs)]

    left, right, right_end, decay_end, vs = [], [], [], [], []
    for ch in range(n_ch):
        rows = slice(ch * t, (ch + 1) * t)
        cs = jnp.dot(tril, lw[rows], precision=HIGHEST, preferred_element_type=F32)
        c_last = cs[t - 1:t, :]
        e_inc, e_exc, e_inv, e_rem = (jnp.exp(cs), jnp.exp(cs - lw[rows]), jnp.exp(-cs),
                                      jnp.exp(c_last - cs))
        left.append(jnp.concatenate([-kkn[rows] * e_exc, r[rows] * e_inc], axis=0))
        right.append(jnp.concatenate([beta[rows] * e_inv, k2[rows] * e_inv], axis=0))
        right_end.append(jnp.concatenate([beta[rows] * e_rem, k2[rows] * e_rem], axis=0))
        decay_end.append(jnp.exp(c_last))
        vs.append(v[rows])

    def pc(x, p):
        return x[:, p * PAIR:(p + 1) * PAIR]

    grams = {}
    for ch, p in units:
        lf, rt = pc(left[ch], p), pc(right[ch], p)
        grams[ch, p, 0] = _dot_nt(jnp.where(lo, lf, 0.0), rt)
        grams[ch, p, 1] = _dot_nt(jnp.where(lo, 0.0, lf), rt)
    keys = [(ch, p, h) for ch, p in units for h in range(2)]
    invs = dict(zip(keys, _tri_inverses(
        [jnp.where(cc < rr, grams[k][:t, :t], 0.0) for k in keys], t)))

    x_keys, lower = {}, {}
    for ch, p in units:
        vv = jnp.concatenate([pc(vs[ch], p)] * 2, axis=0)
        x_keys[ch, p] = jnp.where(lo, _dot(jnp.where(strict_k, grams[ch, p, 0][:t], 0.0), vv),
                                  _dot(jnp.where(strict_k, grams[ch, p, 1][:t], 0.0), vv))
        for h in range(2):
            lower[ch, p, h] = jnp.where(incl2, grams[ch, p, h][t:], 0.0).astype(BF16)

    state = [st_ref[p] for p in range(n_pairs)]
    ys = []
    for ch in range(n_ch):
        y_ch = []
        for p in range(n_pairs):
            from_state = _dot_nt(pc(left[ch], p), state[p])
            x = from_state[:t] + x_keys[ch, p]
            u = jnp.where(lo, _dot(invs[ch, p, 0], x), _dot(invs[ch, p, 1], x))
            uv = jnp.concatenate([u, pc(vs[ch], p)], axis=0)
            y_ch.append(from_state[t:] + jnp.where(lo, _dot(lower[ch, p, 0], uv),
                                                   _dot(lower[ch, p, 1], uv)))
            state[p] = jnp.where(same_head, state[p] * pc(decay_end[ch], p)
                                 + _dot_tn(uv, pc(right_end[ch], p)), 0.0)
        ys.append(jnp.concatenate(y_ch, axis=1))
    for p in range(n_pairs):
        st_ref[p] = state[p]
    return jnp.concatenate(ys, axis=0)


def _rwkv_kernel(r_ref, k_ref, v_ref, l_ref, mur_ref, muk_ref, muv_ref, mul_ref,
                 w0_ref, w2_ref, a0_ref, a2_ref, g2_ref, kk_ref, ka_ref, rk_ref,
                 lnw_ref, lnb_ref, o_ref, st_ref, pr_ref, pk_ref, pv_ref, pl_ref, *, tc, t):
    @pl.when(pl.program_id(1) == 0)
    def _():
        st_ref[...] = jnp.zeros_like(st_ref)
        pr_ref[...] = jnp.zeros_like(pr_ref)
        pk_ref[...] = jnp.zeros_like(pk_ref)
        pv_ref[...] = jnp.zeros_like(pv_ref)
        pl_ref[...] = jnp.zeros_like(pl_ref)

    row = lax.broadcasted_iota(jnp.int32, (tc, 1), 0)

    def shifted(x_ref, p_ref, mu_ref):
        x = x_ref[...]
        prev = jnp.where(row == 0, p_ref[...], pltpu.roll(x, 1, axis=0))
        p_ref[...] = x[tc - 1:tc, :]
        return x + (prev - x) * mu_ref[...]

    r = shifted(r_ref, pr_ref, mur_ref)
    k = shifted(k_ref, pk_ref, muk_ref)
    v = shifted(v_ref, pv_ref, muv_ref)
    lora = shifted(l_ref, pl_ref, mul_ref)
    wa = lora[:, :LANES]
    gl = lora[:, LANES:]
    w_raw = w0_ref[...] + _dot(jnp.tanh(wa), w2_ref[...])
    lw = -math.exp(-0.5) * jax.nn.sigmoid(w_raw)
    a = jax.nn.sigmoid(a0_ref[...] + _dot(wa, a2_ref[...]))
    g = _dot(jax.nn.sigmoid(gl), g2_ref[...])
    kk = k * kk_ref[...]
    k2 = k * (1.0 + (a - 1.0) * ka_ref[...])

    lo = _lo_mask()
    kkn = kk / jnp.maximum(jnp.sqrt(_headsum(kk * kk, lo)), 1e-12)
    bonus = _headsum(r * k2 * rk_ref[...], lo) * v
    y = _rwkv_tile(r, k2, v, kkn, a, lw, st_ref, lo, t)
    mean = _headsum(y, lo) * (1.0 / HEAD_DIM)
    d = y - mean
    var = _headsum(d * d, lo) * (1.0 / HEAD_DIM)
    yn = d * lax.rsqrt(var + RWKV_GN_EPS) * lnw_ref[...] + lnb_ref[...]
    o_ref[...] = (yn + bonus) * g


def _rwkv(z, prm, *, col_r, col_l, width, lora_w, tc):
    batch, seq, _ = z.shape
    n_pairs = width // PAIR
    blk = lambda c: pl.BlockSpec((None, tc, width), lambda b, i: (b, i, c))
    vec = lambda w: pl.BlockSpec((1, w), lambda b, i: (0, 0))
    mat = lambda s: pl.BlockSpec(s, lambda b, i: (0, 0))
    in_specs = [blk(col_r), blk(col_r + 1), blk(col_r + 2),
                pl.BlockSpec((None, tc, lora_w), lambda b, i: (b, i, col_l)),
                vec(width), vec(width), vec(width), vec(lora_w),
                vec(width), mat((LANES, width)), vec(width), mat((LANES, width)),
                mat((lora_w - LANES, width)),
                vec(width), vec(width), vec(width), vec(width), vec(width)]
    return pl.pallas_call(
        functools.partial(_rwkv_kernel, tc=tc, t=RWKV_CHUNK),
        grid=(batch, seq // tc),
        in_specs=in_specs,
        out_specs=pl.BlockSpec((None, tc, width), lambda b, i: (b, i, 0)),
        out_shape=jax.ShapeDtypeStruct((batch, seq, width), F32),
        scratch_shapes=[pltpu.VMEM((n_pairs, PAIR, PAIR), F32),
                        pltpu.VMEM((1, width), F32), pltpu.VMEM((1, width), F32),
                        pltpu.VMEM((1, width), F32), pltpu.VMEM((1, lora_w), F32)],
        compiler_params=_params(("parallel", "arbitrary")),
        name="rwkv7",
    )(z, z, z, z, *prm)


def _xattn_kernel(x_ref, kv_ref, g_ref, wq_ref, wo_ref, qg_ref, kg_ref, o_ref, kn_ref, vb_ref):
    width = XATTN_HEADS * XATTN_HEAD_DIM

    @pl.when(pl.program_id(1) == 0)
    def _():
        for h in range(XATTN_HEADS):
            cols = slice(h * XATTN_HEAD_DIM, (h + 1) * XATTN_HEAD_DIM)
            kn_ref[:, cols] = _rms(kv_ref[:, cols], kg_ref[...]).astype(BF16)
        vb_ref[...] = kv_ref[:, width:].astype(BF16)

    x = x_ref[...]
    q = _dot(_rms(x, g_ref[...]), wq_ref[...])
    scale = XATTN_HEAD_DIM ** -0.5
    outs = []
    for h in range(XATTN_HEADS):
        cols = slice(h * XATTN_HEAD_DIM, (h + 1) * XATTN_HEAD_DIM)
        qh = _rms(q[:, cols], qg_ref[...])
        s = _dot_nt(qh, kn_ref[:, cols]) * scale
        m = jnp.max(s, axis=-1, keepdims=True)
        p = jnp.exp(s - m)
        l = jnp.sum(p, axis=-1, keepdims=True)
        outs.append(_dot(p / l, vb_ref[:, cols]))
    o_ref[...] = x + _dot(jnp.concatenate(outs, axis=-1), wo_ref[...])


def _xattn(x, kv, gain, w_q, w_o, q_gain, k_gain, *, tq):
    batch, seq, d = x.shape
    mem_len = kv.shape[1]
    width = XATTN_HEADS * XATTN_HEAD_DIM
    return pl.pallas_call(
        _xattn_kernel,
        grid=(batch, seq // tq),
        in_specs=[pl.BlockSpec((None, tq, d), lambda b, i: (b, i, 0)),
                  pl.BlockSpec((None, mem_len, 2 * width), lambda b, i: (b, 0, 0)),
                  _resident((1, d)), _resident((d, width)), _resident((width, d)),
                  _resident((1, XATTN_HEAD_DIM)), _resident((1, XATTN_HEAD_DIM))],
        out_specs=pl.BlockSpec((None, tq, d), lambda b, i: (b, i, 0)),
        out_shape=jax.ShapeDtypeStruct((batch, seq, d), F32),
        scratch_shapes=[pltpu.VMEM((mem_len, width), BF16), pltpu.VMEM((mem_len, width), BF16)],
        compiler_params=_params(("parallel", "arbitrary")),
        name="xattn",
    )(x, kv, gain.reshape(1, d), w_q.astype(BF16), w_o.astype(BF16),
      q_gain.reshape(1, -1), k_gain.reshape(1, -1))


def _rope_tables(positions):
    inv_freq = ROPE_THETA ** (-jnp.arange(0, ROPE_DIM, 2, dtype=F32) / ROPE_DIM)
    ang = positions.astype(F32)[..., None] * inv_freq
    cos, sin = jnp.cos(ang), jnp.sin(ang)
    rest = HEAD_DIM - ROPE_DIM
    ones = jnp.ones(cos.shape[:-1] + (rest,), F32)
    zeros = jnp.zeros(cos.shape[:-1] + (rest,), F32)
    cos_f = jnp.concatenate([cos, cos, ones], axis=-1)
    sin_f = jnp.concatenate([-sin, sin, zeros], axis=-1)
    return jnp.tile(cos_f, 2), jnp.tile(sin_f, 2)


def _pad_rows(w, rows, at):
    out = jnp.zeros((rows, w.shape[1]), w.dtype)
    return lax.dynamic_update_slice(out, w, (at, 0))


def _mixers(x, cos_f, sin_f, mix_norm, mix_w_in, fox_f_bias, fox_q_gain, fox_k_gain,
            moba_q_gain, moba_k_gain, rwkv_mu, rwkv_w0, rwkv_w2, rwkv_a0, rwkv_a2, rwkv_g2,
            rwkv_k_k, rwkv_k_a, rwkv_r_k, rwkv_ln_w, rwkv_ln_b, *, tm):
    batch, seq, d = x.shape
    n = batch * seq
    fox_heads = fox_f_bias.shape[0]
    fox_w = fox_heads * HEAD_DIM
    moba_w = fox_w
    rwkv_w = rwkv_w0.shape[0]
    dec_l, icl_l, gate_l = rwkv_w2.shape[0], rwkv_a2.shape[0], rwkv_g2.shape[0]
    assert fox_w % PAIR == 0 and rwkv_w % PAIR == 0 and dec_l + icl_l == LANES
    fox_in = 3 * fox_w + fox_heads
    moba_in = 3 * moba_w
    lora_in = dec_l + icl_l + gate_l
    lora_w = LANES * (-(-(lora_in + fox_heads) // LANES))
    in_width = mix_w_in.shape[1]
    head = 3 * fox_w + moba_in + 3 * rwkv_w
    assert head % lora_w == 0 and head % rwkv_w == 0
    r0 = fox_in + moba_in
    w_perm = jnp.concatenate([
        mix_w_in[:, :3 * fox_w], mix_w_in[:, fox_in:fox_in + moba_in],
        mix_w_in[:, r0:r0 + 3 * rwkv_w], mix_w_in[:, r0 + 3 * rwkv_w:in_width],
        mix_w_in[:, 3 * fox_w:fox_in],
        jnp.zeros((d, lora_w - lora_in - fox_heads), F32)], axis=1)
    total = w_perm.shape[1]
    wf_t = _pad_rows(mix_w_in[:, 3 * fox_w:fox_in].T, 8, 0)

    x2 = x.reshape(n, d)
    tn = total // 9 if total % (9 * LANES) == 0 else LANES
    z2, f_t = _norm_proj_wide(x2, mix_norm, w_perm, wf_t, tm=tm, tn=tn)
    z = z2.reshape(batch, seq, total)

    bias8 = _pad_rows(fox_f_bias.reshape(fox_heads, 1), 8, 0)
    c = _fox_c(f_t, bias8, batch=batch, seq=seq)
    c4 = c[:, :fox_heads].reshape(batch, fox_heads // 2, 2, seq)
    y_fox = _fox_attention(z, c4, fox_q_gain, fox_k_gain, tq=512 if seq % 512 == 0 else 256)
    y_moba = _moba_attention(z, cos_f, sin_f, moba_q_gain, moba_k_gain,
                             col0=3 * fox_w // PAIR, n_pairs=moba_w // PAIR)

    mu_l = jnp.concatenate([rwkv_mu[3 * rwkv_w:], jnp.zeros((lora_w - lora_in,), F32)])
    prm = [rwkv_mu[:rwkv_w], rwkv_mu[rwkv_w:2 * rwkv_w], rwkv_mu[2 * rwkv_w:3 * rwkv_w], mu_l,
           rwkv_w0, _pad_rows(rwkv_w2, LANES, 0).astype(BF16),
           rwkv_a0, _pad_rows(rwkv_a2, LANES, dec_l).astype(BF16),
           _pad_rows(rwkv_g2, lora_w - LANES, 0).astype(BF16),
           rwkv_k_k, rwkv_k_a, rwkv_r_k.reshape(-1), rwkv_ln_w, rwkv_ln_b]
    prm = [p.reshape(1, -1) if p.ndim == 1 else p for p in prm]
    y_rwkv = _rwkv(z, prm, col_r=(3 * fox_w + moba_in) // rwkv_w, col_l=head // lora_w,
                   width=rwkv_w, lora_w=lora_w, tc=256)

    return y_fox, y_rwkv, y_moba


def _mixer_layer(x, cos_f, sin_f, mix_norm, mix_w_in, mix_w_out, *mixer_params, tm):
    batch, seq, d = x.shape
    n = batch * seq
    ys = _mixers(x, cos_f, sin_f, mix_norm, mix_w_in, *mixer_params, tm=tm)
    acts = [y.reshape(n, y.shape[-1]) for y in ys]
    bounds = [0]
    for a in acts:
        bounds.append(bounds[-1] + a.shape[1])
    ws = [mix_w_out[lo:hi] for lo, hi in zip(bounds[:-1], bounds[1:])]
    return _out_proj(x.reshape(n, d), acts, ws, tm=tm).reshape(batch, seq, d)


def kernel(x, mem, positions, ffn1_norm, ffn1_w_in, ffn1_w_out, mix_norm, mix_w_in, mix_w_out, fox_f_bias, fox_q_gain, fox_k_gain, moba_q_gain, moba_k_gain, rwkv_mu, rwkv_w0, rwkv_w2, rwkv_a0, rwkv_a2, rwkv_g2, rwkv_k_k, rwkv_k_a, rwkv_r_k, rwkv_ln_w, rwkv_ln_b, xattn_norm, xattn_mem_norm, xattn_w_q, xattn_w_kv, xattn_q_gain, xattn_k_gain, xattn_w_out, ffn2_norm, ffn2_w_in, ffn2_w_out):
    batch, seq, d = x.shape
    n = batch * seq
    depth = ffn1_norm.shape[0]
    tm = 512 if n % 512 == 0 else 256
    tf = 256
    cos_f, sin_f = _rope_tables(positions)
    mem2 = mem.reshape(-1, d)
    for l in range(depth):
        x = _ffn(x.reshape(n, d), ffn1_norm[l], ffn1_w_in[l], ffn1_w_out[l], tm=tm, tf=tf)
        x = x.reshape(batch, seq, d)
        x = _mixer_layer(x, cos_f, sin_f, mix_norm[l], mix_w_in[l], mix_w_out[l], fox_f_bias[l],
                         fox_q_gain[l], fox_k_gain[l], moba_q_gain[l], moba_k_gain[l], rwkv_mu[l],
                         rwkv_w0[l], rwkv_w2[l], rwkv_a0[l], rwkv_a2[l], rwkv_g2[l], rwkv_k_k[l],
                         rwkv_k_a[l], rwkv_r_k[l], rwkv_ln_w[l], rwkv_ln_b[l], tm=tm)
        kv = _norm_proj(mem2, xattn_mem_norm[l], xattn_w_kv[l], tm=256, tn=512)
        x = _xattn(x, kv.reshape(batch, -1, kv.shape[-1]), xattn_norm[l], xattn_w_q[l],
                   xattn_w_out[l], xattn_q_gain[l], xattn_k_gain[l], tq=512 if seq % 512 == 0 else 256)
        x = _ffn(x.reshape(n, d), ffn2_norm[l], ffn2_w_in[l], ffn2_w_out[l], tm=tm, tf=tf)
        x = x.reshape(batch, seq, d)
    return x
```

```python
import functools
import math

import jax
import jax.numpy as jnp
from jax import lax
from jax.experimental import pallas as pl
from jax.experimental.pallas import tpu as pltpu

F32 = jnp.float32
BF16 = jnp.bfloat16
HIGHEST = lax.Precision.HIGHEST

HEAD_DIM = 64
PAIR = 2 * HEAD_DIM
LANES = 128
NORM_EPS = 1e-6
RWKV_GN_EPS = 64e-5
MOBA_BLOCK = 256
MOBA_TOPK = 3
ROPE_THETA = 500000.0
ROPE_DIM = HEAD_DIM // 4
ROPE_HALF = ROPE_DIM // 2
XATTN_HEADS = 4
XATTN_HEAD_DIM = 128
RWKV_CHUNK = 64
MASKED = -1e30
KEY_SPAN = 1024
LOG2E = math.log2(math.e)
VMEM_LIMIT = 56 * 1024 * 1024


def _params(sem):
    return pltpu.CompilerParams(dimension_semantics=sem, vmem_limit_bytes=VMEM_LIMIT)


def _dot(a, b):
    return jnp.dot(a.astype(BF16), b.astype(BF16), preferred_element_type=F32)


def _dot_nt(a, b):
    return lax.dot_general(a.astype(BF16), b.astype(BF16), (((1,), (1,)), ((), ())),
                           preferred_element_type=F32)


def _dot_tn(a, b):
    return lax.dot_general(a.astype(BF16), b.astype(BF16), (((0,), (0,)), ((), ())),
                           preferred_element_type=F32)


def _resident(shape):
    nd = len(shape)
    return pl.BlockSpec(shape, lambda *_: (0,) * nd, pipeline_mode=pl.Buffered(1))


def _rms(x, gain):
    ms = jnp.mean(x * x, axis=-1, keepdims=True)
    return x * lax.rsqrt(ms + NORM_EPS) * gain


def _lo_mask():
    return lax.broadcasted_iota(jnp.int32, (1, PAIR), 1) < HEAD_DIM


def _halfsum(x, lo):
    s0 = jnp.sum(jnp.where(lo, x, 0.0), axis=-1, keepdims=True)
    s1 = jnp.sum(jnp.where(lo, 0.0, x), axis=-1, keepdims=True)
    return jnp.where(lo, s0, s1)


def _pair_rms(x, gain, lo):
    ms = _halfsum(x * x, lo) * (1.0 / HEAD_DIM)
    return x * lax.rsqrt(ms + NORM_EPS) * gain


def _ffn_kernel(x_ref, g_ref, wg_ref, wu_ref, wo_ref, o_ref, h_ref, acc_ref, *, nf):
    h_ref[...] = _rms(x_ref[...], g_ref[...]).astype(BF16)
    acc_ref[...] = jnp.zeros_like(acc_ref)

    def body(f, carry):
        h = h_ref[...]
        gate = jnp.dot(h, wg_ref[f], preferred_element_type=F32)
        up = jnp.dot(h, wu_ref[f], preferred_element_type=F32)
        act = (gate * jax.nn.sigmoid(gate) * up).astype(BF16)
        acc_ref[...] += jnp.dot(act, wo_ref[f], preferred_element_type=F32)
        return carry

    lax.fori_loop(0, nf, body, 0)
    o_ref[...] = x_ref[...] + 0.5 * acc_ref[...]


def _ffn(x2, gain, w_in, w_out, *, tm, tf):
    n, d = x2.shape
    ffn = w_out.shape[0]
    nf = ffn // tf
    wg = w_in[:, :ffn].astype(BF16).reshape(d, nf, tf).transpose(1, 0, 2)
    wu = w_in[:, ffn:].astype(BF16).reshape(d, nf, tf).transpose(1, 0, 2)
    wo = w_out.astype(BF16).reshape(nf, tf, d)
    return pl.pallas_call(
        functools.partial(_ffn_kernel, nf=nf),
        grid=(n // tm,),
        in_specs=[pl.BlockSpec((tm, d), lambda i: (i, 0)),
                  _resident((1, d)), _resident((nf, d, tf)), _resident((nf, d, tf)),
                  _resident((nf, tf, d))],
        out_specs=pl.BlockSpec((tm, d), lambda i: (i, 0)),
        out_shape=jax.ShapeDtypeStruct((n, d), F32),
        scratch_shapes=[pltpu.VMEM((tm, d), BF16), pltpu.VMEM((tm, d), F32)],
        compiler_params=_params(("parallel",)),
        name="ffn",
    )(x2, gain.reshape(1, d), wg, wu, wo)


def _proj_kernel(x_ref, g_ref, w_ref, o_ref, h_ref):
    @pl.when(pl.program_id(1) == 0)
    def _():
        h_ref[...] = _rms(x_ref[...], g_ref[...]).astype(BF16)

    o_ref[...] = jnp.dot(h_ref[...], w_ref[...], preferred_element_type=F32)


def _proj_t_kernel(x_ref, g_ref, w_ref, wt_ref, o_ref, ot_ref, *, tn):
    h = _rms(x_ref[...], g_ref[...]).astype(BF16)
    ot_ref[...] = _dot_nt(wt_ref[...], h)
    for c in range(0, w_ref.shape[1], tn):
        o_ref[:, c:c + tn] = jnp.dot(h, w_ref[:, c:c + tn], preferred_element_type=F32)


def _norm_proj_wide(x2, gain, w, wt, *, tm, tn):
    n, d = x2.shape
    nc = w.shape[1]
    rows = wt.shape[0]
    return pl.pallas_call(
        functools.partial(_proj_t_kernel, tn=tn),
        grid=(n // tm,),
        in_specs=[pl.BlockSpec((tm, d), lambda i: (i, 0)), _resident((1, d)),
                  _resident((d, nc)), _resident((rows, d))],
        out_specs=[pl.BlockSpec((tm, nc), lambda i: (i, 0)),
                   pl.BlockSpec((rows, tm), lambda i: (0, i))],
        out_shape=[jax.ShapeDtypeStruct((n, nc), F32), jax.ShapeDtypeStruct((rows, n), F32)],
        compiler_params=_params(("parallel",)),
        name="mix_proj",
    )(x2, gain.reshape(1, d), w.astype(BF16), wt.astype(BF16))


def _norm_proj(x2, gain, w, *, tm, tn):
    n, d = x2.shape
    nc = w.shape[1]
    in_specs = [pl.BlockSpec((tm, d), lambda i, j: (i, 0)),
                pl.BlockSpec((1, d), lambda i, j: (0, 0)),
                pl.BlockSpec((d, tn), lambda i, j: (0, j))]
    out_specs = pl.BlockSpec((tm, tn), lambda i, j: (i, j))
    out_shape = jax.ShapeDtypeStruct((n, nc), F32)
    return pl.pallas_call(
        _proj_kernel,
        grid=(n // tm, nc // tn),
        in_specs=in_specs, out_specs=out_specs, out_shape=out_shape,
        scratch_shapes=[pltpu.VMEM((tm, d), BF16)],
        compiler_params=_params(("parallel", "arbitrary")),
        name="norm_proj",
    )(x2, gain.reshape(1, d), w.astype(BF16))


def _outproj_kernel(*refs, n_in):
    x_ref = refs[0]
    a_refs = refs[1:1 + n_in]
    w_refs = refs[1 + n_in:1 + 2 * n_in]
    o_ref = refs[1 + 2 * n_in]
    acc = x_ref[...]
    for a_ref, w_ref in zip(a_refs, w_refs):
        acc = acc + _dot(a_ref[...], w_ref[...])
    o_ref[...] = acc


def _out_proj(x2, acts, ws, *, tm):
    n, d = x2.shape
    n_in = len(acts)
    in_specs = [pl.BlockSpec((tm, d), lambda i: (i, 0))]
    in_specs += [pl.BlockSpec((tm, a.shape[1]), lambda i: (i, 0)) for a in acts]
    in_specs += [_resident(w.shape) for w in ws]
    return pl.pallas_call(
        functools.partial(_outproj_kernel, n_in=n_in),
        grid=(n // tm,),
        in_specs=in_specs,
        out_specs=pl.BlockSpec((tm, d), lambda i: (i, 0)),
        out_shape=jax.ShapeDtypeStruct((n, d), F32),
        compiler_params=_params(("parallel",)),
        name="out_proj",
    )(x2, *acts, *[w.astype(BF16) for w in ws])


def _fox_c_kernel(f_ref, b_ref, c_ref, *, seq):
    z = f_ref[...] + b_ref[...]
    logf = jnp.minimum(z, 0.0) - jnp.log1p(jnp.exp(-jnp.abs(z)))
    r = lax.broadcasted_iota(jnp.int32, (LANES, LANES), 0)
    c = lax.broadcasted_iota(jnp.int32, (LANES, LANES), 1)
    upper = (r <= c).astype(F32)
    carry = jnp.zeros((f_ref.shape[0], 1), F32)
    for ch in range(seq // LANES):
        seg = logf[:, ch * LANES:(ch + 1) * LANES]
        cs = jnp.dot(seg, upper, precision=HIGHEST, preferred_element_type=F32) + carry
        c_ref[:, ch * LANES:(ch + 1) * LANES] = cs
        carry = cs[:, LANES - 1:LANES]


def _fox_c(f_t, bias8, *, batch, seq):
    rows = f_t.shape[0]
    return pl.pallas_call(
        functools.partial(_fox_c_kernel, seq=seq),
        grid=(batch,),
        in_specs=[pl.BlockSpec((rows, seq), lambda b: (0, b)),
                  pl.BlockSpec((rows, 1), lambda b: (0, 0))],
        out_specs=pl.BlockSpec((None, rows, seq), lambda b: (b, 0, 0)),
        out_shape=jax.ShapeDtypeStruct((batch, rows, seq), F32),
        compiler_params=_params(("parallel",)),
        name="fox_c",
    )(f_t, bias8)


def _softmax_block(s, m, acc, v):
    m_new = jnp.maximum(m, jnp.max(s, axis=-1, keepdims=True))
    alpha = jnp.exp2(m - m_new)
    return m_new, alpha * acc + _dot(jnp.exp2(s - m_new), v)


def _softmax_first(s, v):
    m = jnp.max(s, axis=-1, keepdims=True)
    return m, _dot(jnp.exp2(s - m), v)


def _softmax_finish(acc0, acc1, lo):
    return jnp.where(lo, acc0 / pltpu.roll(acc0, HEAD_DIM, axis=1),
                     acc1 / pltpu.roll(acc1, HEAD_DIM, axis=1))


def _store_values(vb_ref, rows, v, lo):
    vb_ref[0, rows, :] = jnp.where(lo, v, 1.0).astype(BF16)
    vb_ref[1, rows, :] = jnp.where(lo, 1.0, v).astype(BF16)


def _fox_kernel(q_ref, k_ref, v_ref, c_ref, qg_ref, kg_ref, o_ref, kn_ref, vb_ref, *, tq, seq, tks):
    i = pl.program_id(2)
    lo = _lo_mask()

    @pl.when(i == 0)
    def _():
        def prep(j, carry):
            rows = pl.ds(pl.multiple_of(j * tq, tq), tq)
            kn_ref[rows, :] = _pair_rms(k_ref[rows, :], kg_ref[...], lo).astype(BF16)
            _store_values(vb_ref, rows, v_ref[rows, :], lo)
            return carry
        lax.fori_loop(0, seq // tq, prep, 0)

    q = _pair_rms(q_ref[...], qg_ref[...], lo)
    qh = (jnp.where(lo, q, 0.0).astype(BF16), jnp.where(lo, 0.0, q).astype(BF16))

    start = pl.multiple_of(i * tq, tq)
    c_first = c_ref[:, pl.ds(start, tq)][:, 0:1]
    n_full = (i * tq) // tks
    rem = pl.multiple_of(n_full * tks, tks)

    k_r = kn_ref[pl.ds(rem, tks), :]
    q_pos = start + lax.broadcasted_iota(jnp.int32, (tq, 1), 0)
    k_pos = rem + lax.broadcasted_iota(jnp.int32, (1, tks), 1)
    visible = k_pos <= q_pos
    bias_r = (c_first - c_ref[:, pl.ds(rem, tks)]) * LOG2E
    state = []
    for h in range(2):
        s = _dot_nt(qh[h], k_r) + bias_r[h:h + 1, :]
        state.extend(_softmax_first(jnp.where(visible, s, MASKED), vb_ref[h, pl.ds(rem, tks), :]))

    def body(j, st):
        off = pl.multiple_of(j * tks, tks)
        kj = kn_ref[pl.ds(off, tks), :]
        bias = (c_first - c_ref[:, pl.ds(off, tks)]) * LOG2E
        out = []
        for h in range(2):
            s = _dot_nt(qh[h], kj) + bias[h:h + 1, :]
            out.extend(_softmax_block(s, st[2 * h], st[2 * h + 1], vb_ref[h, pl.ds(off, tks), :]))
        return tuple(out)

    _, a0, _, a1 = lax.fori_loop(0, n_full, body, tuple(state))
    o_ref[...] = _softmax_finish(a0, a1, lo)


def _fox_attention(z, c4, q_gain, k_gain, *, tq):
    batch, seq, _ = z.shape
    n_pairs = c4.shape[1]
    scale = HEAD_DIM ** -0.5 * LOG2E
    qg = jnp.tile(q_gain, 2).reshape(1, PAIR) * scale
    kg = jnp.tile(k_gain, 2).reshape(1, PAIR)
    tks = min(KEY_SPAN, seq)
    assert seq % tks == 0 and tks % tq == 0
    return pl.pallas_call(
        functools.partial(_fox_kernel, tq=tq, seq=seq, tks=tks),
        grid=(batch, n_pairs, seq // tq),
        in_specs=[pl.BlockSpec((None, tq, PAIR), lambda b, p, i: (b, i, p)),
                  pl.BlockSpec((None, seq, PAIR), lambda b, p, i: (b, 0, n_pairs + p)),
                  pl.BlockSpec((None, seq, PAIR), lambda b, p, i: (b, 0, 2 * n_pairs + p)),
                  pl.BlockSpec((None, None, 2, seq), lambda b, p, i: (b, p, 0, 0)),
                  pl.BlockSpec((1, PAIR), lambda b, p, i: (0, 0)),
                  pl.BlockSpec((1, PAIR), lambda b, p, i: (0, 0))],
        out_specs=pl.BlockSpec((None, tq, PAIR), lambda b, p, i: (b, i, p)),
        out_shape=jax.ShapeDtypeStruct((batch, seq, n_pairs * PAIR), F32),
        scratch_shapes=[pltpu.VMEM((seq, PAIR), BF16), pltpu.VMEM((2, seq, PAIR), BF16)],
        compiler_params=_params(("parallel", "parallel", "arbitrary")),
        name="fox_attn",
    )(z, z, z, c4, qg, kg)


def _rope(x, cos_f, sin_f):
    lane = lax.broadcasted_iota(jnp.int32, (1, PAIR), 1) % HEAD_DIM
    partner = jnp.where(lane < ROPE_HALF,
                        pltpu.roll(x, PAIR - ROPE_HALF, axis=1),
                        pltpu.roll(x, ROPE_HALF, axis=1))
    return x * cos_f + partner * sin_f


def _topk_mask(gate_t, n_valid):
    blk = lax.broadcasted_iota(jnp.int32, gate_t.shape, 0)
    blk_f = blk.astype(F32)
    neg = -jnp.inf
    g = jnp.where(blk < n_valid, gate_t, neg)
    sel = jnp.zeros(gate_t.shape, F32)
    for _ in range(MOBA_TOPK):
        mx = jnp.max(g, axis=0, keepdims=True)
        first = jnp.min(jnp.where(g == mx, blk_f, float(LANES)), axis=0, keepdims=True)
        pick = jnp.logical_and(blk_f == first, mx > neg)
        sel = jnp.where(pick, 1.0, sel)
        g = jnp.where(pick, neg, g)
    return sel


def _moba_kernel(q_ref, k_ref, v_ref, cq_ref, sq_ref, ck_ref, sk_ref, qg_ref, kg_ref,
                 o_ref, kn_ref, vb_ref, km_ref, *, tq, tb, seq, tks):
    i = pl.program_id(2)
    lo = _lo_mask()
    n_blk = seq // tb

    @pl.when(i == 0)
    def _():
        km_ref[...] = jnp.zeros_like(km_ref)
        for n in range(n_blk):
            rows = pl.ds(n * tb, tb)
            kn = _rope(_pair_rms(k_ref[rows, :], kg_ref[...], lo), ck_ref[rows, :], sk_ref[rows, :])
            kn_ref[rows, :] = kn.astype(BF16)
            _store_values(vb_ref, rows, v_ref[rows, :], lo)
            km_ref[n:n + 1, :] = jnp.mean(kn, axis=0, keepdims=True)

    q = _rope(_pair_rms(q_ref[...], qg_ref[...], lo), cq_ref[...], sq_ref[...])
    q_lo = jnp.where(lo, q, 0.0)
    q_hi = jnp.where(lo, 0.0, q)
    km = km_ref[...]
    scale = HEAD_DIM ** -0.5 * LOG2E
    rr = lax.broadcasted_iota(jnp.int32, (tq, 1), 0)
    sub = jnp.zeros((tq, 1), jnp.int32)
    for edge in range(tb, tq, tb):
        sub = sub + (rr >= edge).astype(jnp.int32)
    q_blk = i * (tq // tb) + sub
    row_in_blk = rr - sub * tb

    col = lax.broadcasted_iota(jnp.int32, (1, tq), 1)
    q_blk_row = jnp.full((1, tq), i * (tq // tb), jnp.int32)
    for edge in range(tb, tq, tb):
        q_blk_row = q_blk_row + (col >= edge).astype(jnp.int32)
    nb8 = -(-n_blk // 8) * 8
    to_lanes = (lax.broadcasted_iota(jnp.int32, (nb8, LANES), 0)
                == lax.broadcasted_iota(jnp.int32, (nb8, LANES), 1)).astype(F32)
    qh, sel = [], []
    for qm in (q_lo, q_hi):
        gate_t = lax.dot_general(km[:nb8], qm, (((1,), (1,)), ((), ())), precision=HIGHEST,
                                 preferred_element_type=F32)
        sel.append(_dot_tn(_topk_mask(gate_t, q_blk_row), to_lanes))
        qh.append((qm * scale).astype(BF16))

    per_span = tks // tb
    n_full = (i * tq) // tks
    lane = lax.broadcasted_iota(jnp.int32, (1, LANES), 1)
    cc = lax.broadcasted_iota(jnp.int32, (1, tb), 1)

    def masked_scores(h, span, k_span, with_own):
        s = _dot_nt(qh[h], k_span)
        parts = []
        for b in range(per_span):
            blk = span * per_span + b
            chosen = jnp.max(jnp.where(lane == blk, sel[h], 0.0), axis=-1, keepdims=True)
            keep = chosen > 0.0
            if with_own:
                last_visible = jnp.where(q_blk == blk, row_in_blk, -1)
                keep = jnp.logical_or(keep, cc <= last_visible)
            parts.append(jnp.where(keep, s[:, b * tb:(b + 1) * tb], MASKED))
        return jnp.concatenate(parts, axis=1)

    rem = pl.multiple_of(n_full * tks, tks)
    k_r = kn_ref[pl.ds(rem, tks), :]
    state = []
    for h in range(2):
        state.extend(_softmax_first(masked_scores(h, n_full, k_r, True),
                                    vb_ref[h, pl.ds(rem, tks), :]))

    def body(j, st):
        off = pl.multiple_of(j * tks, tks)
        kj = kn_ref[pl.ds(off, tks), :]
        out = []
        for h in range(2):
            s = masked_scores(h, j, kj, False)
            out.extend(_softmax_block(s, st[2 * h], st[2 * h + 1], vb_ref[h, pl.ds(off, tks), :]))
        return tuple(out)

    _, a0, _, a1 = lax.fori_loop(0, n_full, body, tuple(state))
    o_ref[...] = _softmax_finish(a0, a1, lo)


def _moba_attention(z, cos_f, sin_f, q_gain, k_gain, *, col0, n_pairs):
    batch, seq, _ = z.shape
    tb = MOBA_BLOCK
    tq = 2 * tb if seq % (2 * tb) == 0 else tb
    assert seq % tq == 0 and seq // tb <= LANES
    qg = jnp.tile(q_gain, 2).reshape(1, PAIR)
    kg = jnp.tile(k_gain, 2).reshape(1, PAIR)
    tile = pl.BlockSpec((None, tq, PAIR), lambda b, p, i: (b, i, 0))
    full = pl.BlockSpec((None, seq, PAIR), lambda b, p, i: (b, 0, 0))
    tks = min(KEY_SPAN, seq)
    assert seq % tks == 0 and tks % tq == 0
    return pl.pallas_call(
        functools.partial(_moba_kernel, tq=tq, tb=tb, seq=seq, tks=tks),
        grid=(batch, n_pairs, seq // tq),
        in_specs=[pl.BlockSpec((None, tq, PAIR), lambda b, p, i: (b, i, col0 + p)),
                  pl.BlockSpec((None, seq, PAIR), lambda b, p, i: (b, 0, col0 + n_pairs + p)),
                  pl.BlockSpec((None, seq, PAIR), lambda b, p, i: (b, 0, col0 + 2 * n_pairs + p)),
                  tile, tile, full, full,
                  pl.BlockSpec((1, PAIR), lambda b, p, i: (0, 0)),
                  pl.BlockSpec((1, PAIR), lambda b, p, i: (0, 0))],
        out_specs=pl.BlockSpec((None, tq, PAIR), lambda b, p, i: (b, i, p)),
        out_shape=jax.ShapeDtypeStruct((batch, seq, n_pairs * PAIR), F32),
        scratch_shapes=[pltpu.VMEM((seq, PAIR), BF16), pltpu.VMEM((2, seq, PAIR), BF16),
                        pltpu.VMEM((LANES, PAIR), F32)],
        compiler_params=_params(("parallel", "parallel", "arbitrary")),
        name="moba_attn",
    )(z, z, z, cos_f, sin_f, cos_f, sin_f, qg, kg)


def _tri_inverses(n_mats, t):
    r = lax.broadcasted_iota(jnp.int32, (t, t), 0)
    c = lax.broadcasted_iota(jnp.int32, (t, t), 1)
    eye = (r == c).astype(F32)
    ms = [eye + n for n in n_mats]
    ps = list(n_mats)
    for _ in range(int(math.log2(t)) - 1):
        ps = [_dot(p, p) for p in ps]
        ms = [m + _dot(m, p) for m, p in zip(ms, ps)]
    return ms


def _headsum(x, lo):
    return jnp.concatenate([_halfsum(x[:, c:c + PAIR], lo) for c in range(0, x.shape[1], PAIR)],
                           axis=1)


def _rwkv_tile(r, k2, v, kkn, a, lw, st_ref, lo, t):
    tc, width = r.shape
    n_ch, n_pairs = tc // t, width // PAIR
    rr = lax.broadcasted_iota(jnp.int32, (t, t), 0)
    cc = lax.broadcasted_iota(jnp.int32, (t, t), 1)
    tril = (cc <= rr).astype(F32)
    r2 = lax.broadcasted_iota(jnp.int32, (t, 2 * t), 0)
    c2 = lax.broadcasted_iota(jnp.int32, (t, 2 * t), 1)
    strict_k = jnp.logical_and(c2 >= t, c2 - t < r2)
    incl2 = jnp.where(c2 >= t, c2 - t, c2) <= r2
    same_head = (lax.broadcasted_iota(jnp.int32, (PAIR, PAIR), 0) < HEAD_DIM) == (
        lax.broadcasted_iota(jnp.int32, (PAIR, PAIR), 1) < HEAD_DIM)
    beta = kkn * a
    units = [(ch, p) for ch in range(n_ch) for p in range(n_pairs)]

    left, right, right_end, decay_end, vs = [], [], [], [], []
    for ch in range(n_ch):
        rows = slice(ch * t, (ch + 1) * t)
        cs = jnp.dot(tril, lw[rows], precision=HIGHEST, preferred_element_type=F32)
        c_last = cs[t - 1:t, :]
        e_inc, e_exc, e_inv, e_rem = (jnp.exp(cs), jnp.exp(cs - lw[rows]), jnp.exp(-cs),
                                      jnp.exp(c_last - cs))
        left.append(jnp.concatenate([-kkn[rows] * e_exc, r[rows] * e_inc], axis=0))
        right.append(jnp.concatenate([beta[rows] * e_inv, k2[rows] * e_inv], axis=0))
        right_end.append(jnp.concatenate([beta[rows] * e_rem, k2[rows] * e_rem], axis=0))
        decay_end.append(jnp.exp(c_last))
        vs.append(v[rows])

    def pc(x, p):
        return x[:, p * PAIR:(p + 1) * PAIR]

    grams = {}
    for ch, p in units:
        lf, rt = pc(left[ch], p), pc(right[ch], p)
        grams[ch, p, 0] = _dot_nt(jnp.where(lo, lf, 0.0), rt)
        grams[ch, p, 1] = _dot_nt(jnp.where(lo, 0.0, lf), rt)
    keys = [(ch, p, h) for ch, p in units for h in range(2)]
    invs = dict(zip(keys, _tri_inverses(
        [jnp.where(cc < rr, grams[k][:t, :t], 0.0) for k in keys], t)))

    x_keys, lower = {}, {}
    for ch, p in units:
        vv = jnp.concatenate([pc(vs[ch], p)] * 2, axis=0)
        x_keys[ch, p] = jnp.where(lo, _dot(jnp.where(strict_k, grams[ch, p, 0][:t], 0.0), vv),
                                  _dot(jnp.where(strict_k, grams[ch, p, 1][:t], 0.0), vv))
        for h in range(2):
            lower[ch, p, h] = jnp.where(incl2, grams[ch, p, h][t:], 0.0).astype(BF16)

    state = [st_ref[p] for p in range(n_pairs)]
    ys = []
    for ch in range(n_ch):
        y_ch = []
        for p in range(n_pairs):
            from_state = _dot_nt(pc(left[ch], p), state[p])
            x = from_state[:t] + x_keys[ch, p]
            u = jnp.where(lo, _dot(invs[ch, p, 0], x), _dot(invs[ch, p, 1], x))
            uv = jnp.concatenate([u, pc(vs[ch], p)], axis=0)
            y_ch.append(from_state[t:] + jnp.where(lo, _dot(lower[ch, p, 0], uv),
                                                   _dot(lower[ch, p, 1], uv)))
            state[p] = jnp.where(same_head, state[p] * pc(decay_end[ch], p)
                                 + _dot_tn(uv, pc(right_end[ch], p)), 0.0)
        ys.append(jnp.concatenate(y_ch, axis=1))
    for p in range(n_pairs):
        st_ref[p] = state[p]
    return jnp.concatenate(ys, axis=0)


def _rwkv_kernel(r_ref, k_ref, v_ref, l_ref, mur_ref, muk_ref, muv_ref, mul_ref,
                 w0_ref, w2_ref, a0_ref, a2_ref, g2_ref, kk_ref, ka_ref, rk_ref,
                 lnw_ref, lnb_ref, o_ref, st_ref, pr_ref, pk_ref, pv_ref, pl_ref, *, tc, t):
    @pl.when(pl.program_id(1) == 0)
    def _():
        st_ref[...] = jnp.zeros_like(st_ref)
        pr_ref[...] = jnp.zeros_like(pr_ref)
        pk_ref[...] = jnp.zeros_like(pk_ref)
        pv_ref[...] = jnp.zeros_like(pv_ref)
        pl_ref[...] = jnp.zeros_like(pl_ref)

    row = lax.broadcasted_iota(jnp.int32, (tc, 1), 0)

    def shifted(x_ref, p_ref, mu_ref):
        x = x_ref[...]
        prev = jnp.where(row == 0, p_ref[...], pltpu.roll(x, 1, axis=0))
        p_ref[...] = x[tc - 1:tc, :]
        return x + (prev - x) * mu_ref[...]

    r = shifted(r_ref, pr_ref, mur_ref)
    k = shifted(k_ref, pk_ref, muk_ref)
    v = shifted(v_ref, pv_ref, muv_ref)
    lora = shifted(l_ref, pl_ref, mul_ref)
    wa = lora[:, :LANES]
    gl = lora[:, LANES:]
    w_raw = w0_ref[...] + _dot(jnp.tanh(wa), w2_ref[...])
    lw = -math.exp(-0.5) * jax.nn.sigmoid(w_raw)
    a = jax.nn.sigmoid(a0_ref[...] + _dot(wa, a2_ref[...]))
    g = _dot(jax.nn.sigmoid(gl), g2_ref[...])
    kk = k * kk_ref[...]
    k2 = k * (1.0 + (a - 1.0) * ka_ref[...])

    lo = _lo_mask()
    kkn = kk / jnp.maximum(jnp.sqrt(_headsum(kk * kk, lo)), 1e-12)
    bonus = _headsum(r * k2 * rk_ref[...], lo) * v
    y = _rwkv_tile(r, k2, v, kkn, a, lw, st_ref, lo, t)
    mean = _headsum(y, lo) * (1.0 / HEAD_DIM)
    d = y - mean
    var = _headsum(d * d, lo) * (1.0 / HEAD_DIM)
    yn = d * lax.rsqrt(var + RWKV_GN_EPS) * lnw_ref[...] + lnb_ref[...]
    o_ref[...] = (yn + bonus) * g


def _rwkv(z, prm, *, col_r, col_l, width, lora_w, tc):
    batch, seq, _ = z.shape
    n_pairs = width // PAIR
    blk = lambda c: pl.BlockSpec((None, tc, width), lambda b, i: (b, i, c))
    vec = lambda w: pl.BlockSpec((1, w), lambda b, i: (0, 0))
    mat = lambda s: pl.BlockSpec(s, lambda b, i: (0, 0))
    in_specs = [blk(col_r), blk(col_r + 1), blk(col_r + 2),
                pl.BlockSpec((None, tc, lora_w), lambda b, i: (b, i, col_l)),
                vec(width), vec(width), vec(width), vec(lora_w),
                vec(width), mat((LANES, width)), vec(width), mat((LANES, width)),
                mat((lora_w - LANES, width)),
                vec(width), vec(width), vec(width), vec(width), vec(width)]
    return pl.pallas_call(
        functools.partial(_rwkv_kernel, tc=tc, t=RWKV_CHUNK),
        grid=(batch, seq // tc),
        in_specs=in_specs,
        out_specs=pl.BlockSpec((None, tc, width), lambda b, i: (b, i, 0)),
        out_shape=jax.ShapeDtypeStruct((batch, seq, width), F32),
        scratch_shapes=[pltpu.VMEM((n_pairs, PAIR, PAIR), F32),
                        pltpu.VMEM((1, width), F32), pltpu.VMEM((1, width), F32),
                        pltpu.VMEM((1, width), F32), pltpu.VMEM((1, lora_w), F32)],
        compiler_params=_params(("parallel", "arbitrary")),
        name="rwkv7",
    )(z, z, z, z, *prm)


def _xattn_kernel(x_ref, kv_ref, g_ref, wq_ref, wo_ref, qg_ref, kg_ref, o_ref, kn_ref, vb_ref):
    width = XATTN_HEADS * XATTN_HEAD_DIM

    @pl.when(pl.program_id(1) == 0)
    def _():
        for h in range(XATTN_HEADS):
            cols = slice(h * XATTN_HEAD_DIM, (h + 1) * XATTN_HEAD_DIM)
            kn_ref[:, cols] = _rms(kv_ref[:, cols], kg_ref[...]).astype(BF16)
        vb_ref[...] = kv_ref[:, width:].astype(BF16)

    x = x_ref[...]
    q = _dot(_rms(x, g_ref[...]), wq_ref[...])
    scale = XATTN_HEAD_DIM ** -0.5
    outs = []
    for h in range(XATTN_HEADS):
        cols = slice(h * XATTN_HEAD_DIM, (h + 1) * XATTN_HEAD_DIM)
        qh = _rms(q[:, cols], qg_ref[...])
        s = _dot_nt(qh, kn_ref[:, cols]) * scale
        m = jnp.max(s, axis=-1, keepdims=True)
        p = jnp.exp(s - m)
        l = jnp.sum(p, axis=-1, keepdims=True)
        outs.append(_dot(p / l, vb_ref[:, cols]))
    o_ref[...] = x + _dot(jnp.concatenate(outs, axis=-1), wo_ref[...])


def _xattn(x, kv, gain, w_q, w_o, q_gain, k_gain, *, tq):
    batch, seq, d = x.shape
    mem_len = kv.shape[1]
    width = XATTN_HEADS * XATTN_HEAD_DIM
    return pl.pallas_call(
        _xattn_kernel,
        grid=(batch, seq // tq),
        in_specs=[pl.BlockSpec((None, tq, d), lambda b, i: (b, i, 0)),
                  pl.BlockSpec((None, mem_len, 2 * width), lambda b, i: (b, 0, 0)),
                  _resident((1, d)), _resident((d, width)), _resident((width, d)),
                  _resident((1, XATTN_HEAD_DIM)), _resident((1, XATTN_HEAD_DIM))],
        out_specs=pl.BlockSpec((None, tq, d), lambda b, i: (b, i, 0)),
        out_shape=jax.ShapeDtypeStruct((batch, seq, d), F32),
        scratch_shapes=[pltpu.VMEM((mem_len, width), BF16), pltpu.VMEM((mem_len, width), BF16)],
        compiler_params=_params(("parallel", "arbitrary")),
        name="xattn",
    )(x, kv, gain.reshape(1, d), w_q.astype(BF16), w_o.astype(BF16),
      q_gain.reshape(1, -1), k_gain.reshape(1, -1))


def _rope_tables(positions):
    inv_freq = ROPE_THETA ** (-jnp.arange(0, ROPE_DIM, 2, dtype=F32) / ROPE_DIM)
    ang = positions.astype(F32)[..., None] * inv_freq
    cos, sin = jnp.cos(ang), jnp.sin(ang)
    rest = HEAD_DIM - ROPE_DIM
    ones = jnp.ones(cos.shape[:-1] + (rest,), F32)
    zeros = jnp.zeros(cos.shape[:-1] + (rest,), F32)
    cos_f = jnp.concatenate([cos, cos, ones], axis=-1)
    sin_f = jnp.concatenate([-sin, sin, zeros], axis=-1)
    return jnp.tile(cos_f, 2), jnp.tile(sin_f, 2)


def _pad_rows(w, rows, at):
    out = jnp.zeros((rows, w.shape[1]), w.dtype)
    return lax.dynamic_update_slice(out, w, (at, 0))


def _mixers(x, cos_f, sin_f, mix_norm, mix_w_in, fox_f_bias, fox_q_gain, fox_k_gain,
            moba_q_gain, moba_k_gain, rwkv_mu, rwkv_w0, rwkv_w2, rwkv_a0, rwkv_a2, rwkv_g2,
            rwkv_k_k, rwkv_k_a, rwkv_r_k, rwkv_ln_w, rwkv_ln_b, *, tm):
    batch, seq, d = x.shape
    n = batch * seq
    fox_heads = fox_f_bias.shape[0]
    fox_w = fox_heads * HEAD_DIM
    moba_w = fox_w
    rwkv_w = rwkv_w0.shape[0]
    dec_l, icl_l, gate_l = rwkv_w2.shape[0], rwkv_a2.shape[0], rwkv_g2.shape[0]
    assert fox_w % PAIR == 0 and rwkv_w % PAIR == 0 and dec_l + icl_l == LANES
    fox_in = 3 * fox_w + fox_heads
    moba_in = 3 * moba_w
    lora_in = dec_l + icl_l + gate_l
    lora_w = LANES * (-(-(lora_in + fox_heads) // LANES))
    in_width = mix_w_in.shape[1]
    head = 3 * fox_w + moba_in + 3 * rwkv_w
    assert head % lora_w == 0 and head % rwkv_w == 0
    r0 = fox_in + moba_in
    w_perm = jnp.concatenate([
        mix_w_in[:, :3 * fox_w], mix_w_in[:, fox_in:fox_in + moba_in],
        mix_w_in[:, r0:r0 + 3 * rwkv_w], mix_w_in[:, r0 + 3 * rwkv_w:in_width],
        mix_w_in[:, 3 * fox_w:fox_in],
        jnp.zeros((d, lora_w - lora_in - fox_heads), F32)], axis=1)
    total = w_perm.shape[1]
    wf_t = _pad_rows(mix_w_in[:, 3 * fox_w:fox_in].T, 8, 0)

    x2 = x.reshape(n, d)
    tn = total // 9 if total % (9 * LANES) == 0 else LANES
    z2, f_t = _norm_proj_wide(x2, mix_norm, w_perm, wf_t, tm=tm, tn=tn)
    z = z2.reshape(batch, seq, total)

    bias8 = _pad_rows(fox_f_bias.reshape(fox_heads, 1), 8, 0)
    c = _fox_c(f_t, bias8, batch=batch, seq=seq)
    c4 = c[:, :fox_heads].reshape(batch, fox_heads // 2, 2, seq)
    y_fox = _fox_attention(z, c4, fox_q_gain, fox_k_gain, tq=512 if seq % 512 == 0 else 256)
    y_moba = _moba_attention(z, cos_f, sin_f, moba_q_gain, moba_k_gain,
                             col0=3 * fox_w // PAIR, n_pairs=moba_w // PAIR)

    mu_l = jnp.concatenate([rwkv_mu[3 * rwkv_w:], jnp.zeros((lora_w - lora_in,), F32)])
    prm = [rwkv_mu[:rwkv_w], rwkv_mu[rwkv_w:2 * rwkv_w], rwkv_mu[2 * rwkv_w:3 * rwkv_w], mu_l,
           rwkv_w0, _pad_rows(rwkv_w2, LANES, 0).astype(BF16),
           rwkv_a0, _pad_rows(rwkv_a2, LANES, dec_l).astype(BF16),
           _pad_rows(rwkv_g2, lora_w - LANES, 0).astype(BF16),
           rwkv_k_k, rwkv_k_a, rwkv_r_k.reshape(-1), rwkv_ln_w, rwkv_ln_b]
    prm = [p.reshape(1, -1) if p.ndim == 1 else p for p in prm]
    y_rwkv = _rwkv(z, prm, col_r=(3 * fox_w + moba_in) // rwkv_w, col_l=head // lora_w,
                   width=rwkv_w, lora_w=lora_w, tc=256)

    return y_fox, y_rwkv, y_moba


def _mixer_layer(x, cos_f, sin_f, mix_norm, mix_w_in, mix_w_out, *mixer_params, tm):
    batch, seq, d = x.shape
    n = batch * seq
    ys = _mixers(x, cos_f, sin_f, mix_norm, mix_w_in, *mixer_params, tm=tm)
    acts = [y.reshape(n, y.shape[-1]) for y in ys]
    bounds = [0]
    for a in acts:
        bounds.append(bounds[-1] + a.shape[1])
    ws = [mix_w_out[lo:hi] for lo, hi in zip(bounds[:-1], bounds[1:])]
    return _out_proj(x.reshape(n, d), acts, ws, tm=tm).reshape(batch, seq, d)


def kernel(x, mem, positions, ffn1_norm, ffn1_w_in, ffn1_w_out, mix_norm, mix_w_in, mix_w_out, fox_f_bias, fox_q_gain, fox_k_gain, moba_q_gain, moba_k_gain, rwkv_mu, rwkv_w0, rwkv_w2, rwkv_a0, rwkv_a2, rwkv_g2, rwkv_k_k, rwkv_k_a, rwkv_r_k, rwkv_ln_w, rwkv_ln_b, xattn_norm, xattn_mem_norm, xattn_w_q, xattn_w_kv, xattn_q_gain, xattn_k_gain, xattn_w_out, ffn2_norm, ffn2_w_in, ffn2_w_out):
    batch, seq, d = x.shape
    n = batch * seq
    depth = ffn1_norm.shape[0]
    tm = 512 if n % 512 == 0 else 256
    tf = 1408
    cos_f, sin_f = _rope_tables(positions)
    mem2 = mem.reshape(-1, d)
    for l in range(depth):
        x = _ffn(x.reshape(n, d), ffn1_norm[l], ffn1_w_in[l], ffn1_w_out[l], tm=tm, tf=tf)
        x = x.reshape(batch, seq, d)
        x = _mixer_layer(x, cos_f, sin_f, mix_norm[l], mix_w_in[l], mix_w_out[l], fox_f_bias[l],
                         fox_q_gain[l], fox_k_gain[l], moba_q_gain[l], moba_k_gain[l], rwkv_mu[l],
                         rwkv_w0[l], rwkv_w2[l], rwkv_a0[l], rwkv_a2[l], rwkv_g2[l], rwkv_k_k[l],
                         rwkv_k_a[l], rwkv_r_k[l], rwkv_ln_w[l], rwkv_ln_b[l], tm=tm)
        kv = _norm_proj(mem2, xattn_mem_norm[l], xattn_w_kv[l], tm=256, tn=512)
        x = _xattn(x, kv.reshape(batch, -1, kv.shape[-1]), xattn_norm[l], xattn_w_q[l],
                   xattn_w_out[l], xattn_q_gain[l], xattn_k_gain[l], tq=512 if seq % 512 == 0 else 256)
        x = _ffn(x.reshape(n, d), ffn2_norm[l], ffn2_w_in[l], ffn2_w_out[l], tm=tm, tf=tf)
        x = x.reshape(batch, seq, d)
    return x
```

```python
import functools
import math

import jax
import jax.numpy as jnp
from jax import lax
from jax.experimental import pallas as pl
from jax.experimental.pallas import tpu as pltpu

F32 = jnp.float32
BF16 = jnp.bfloat16
HIGHEST = lax.Precision.HIGHEST

HEAD_DIM = 64
PAIR = 2 * HEAD_DIM
LANES = 128
NORM_EPS = 1e-6
RWKV_GN_EPS = 64e-5
MOBA_BLOCK = 256
MOBA_TOPK = 3
ROPE_THETA = 500000.0
ROPE_DIM = HEAD_DIM // 4
ROPE_HALF = ROPE_DIM // 2
XATTN_HEADS = 4
XATTN_HEAD_DIM = 128
RWKV_CHUNK = 64
MASKED = -1e30
KEY_SPAN = 1024
LOG2E = math.log2(math.e)
VMEM_LIMIT = 56 * 1024 * 1024


def _params(sem):
    return pltpu.CompilerParams(dimension_semantics=sem, vmem_limit_bytes=VMEM_LIMIT)


def _dot(a, b):
    return jnp.dot(a.astype(BF16), b.astype(BF16), preferred_element_type=F32)


def _dot_nt(a, b):
    return lax.dot_general(a.astype(BF16), b.astype(BF16), (((1,), (1,)), ((), ())),
                           preferred_element_type=F32)


def _dot_tn(a, b):
    return lax.dot_general(a.astype(BF16), b.astype(BF16), (((0,), (0,)), ((), ())),
                           preferred_element_type=F32)


def _resident(shape):
    nd = len(shape)
    return pl.BlockSpec(shape, lambda *_: (0,) * nd, pipeline_mode=pl.Buffered(1))


def _rms(x, gain):
    ms = jnp.mean(x * x, axis=-1, keepdims=True)
    return x * lax.rsqrt(ms + NORM_EPS) * gain


def _lo_mask():
    return lax.broadcasted_iota(jnp.int32, (1, PAIR), 1) < HEAD_DIM


def _halfsum(x, lo):
    s0 = jnp.sum(jnp.where(lo, x, 0.0), axis=-1, keepdims=True)
    s1 = jnp.sum(jnp.where(lo, 0.0, x), axis=-1, keepdims=True)
    return jnp.where(lo, s0, s1)


def _pair_rms(x, gain, lo):
    ms = _halfsum(x * x, lo) * (1.0 / HEAD_DIM)
    return x * lax.rsqrt(ms + NORM_EPS) * gain


def _ffn_kernel(x_ref, g_ref, wg_ref, wu_ref, wo_ref, o_ref, h_ref, acc_ref, *, nf):
    h_ref[...] = _rms(x_ref[...], g_ref[...]).astype(BF16)
    acc_ref[...] = jnp.zeros_like(acc_ref)

    def body(f, carry):
        h = h_ref[...]
        gate = jnp.dot(h, wg_ref[f], preferred_element_type=F32)
        up = jnp.dot(h, wu_ref[f], preferred_element_type=F32)
        act = (gate * jax.nn.sigmoid(gate) * up).astype(BF16)
        acc_ref[...] += jnp.dot(act, wo_ref[f], preferred_element_type=F32)
        return carry

    lax.fori_loop(0, nf, body, 0)
    o_ref[...] = x_ref[...] + 0.5 * acc_ref[...]


def _ffn(x2, gain, w_in, w_out, *, tm, tf):
    n, d = x2.shape
    ffn = w_out.shape[0]
    nf = ffn // tf
    wg = w_in[:, :ffn].astype(BF16).reshape(d, nf, tf).transpose(1, 0, 2)
    wu = w_in[:, ffn:].astype(BF16).reshape(d, nf, tf).transpose(1, 0, 2)
    wo = w_out.astype(BF16).reshape(nf, tf, d)
    return pl.pallas_call(
        functools.partial(_ffn_kernel, nf=nf),
        grid=(n // tm,),
        in_specs=[pl.BlockSpec((tm, d), lambda i: (i, 0)),
                  _resident((1, d)), _resident((nf, d, tf)), _resident((nf, d, tf)),
                  _resident((nf, tf, d))],
        out_specs=pl.BlockSpec((tm, d), lambda i: (i, 0)),
        out_shape=jax.ShapeDtypeStruct((n, d), F32),
        scratch_shapes=[pltpu.VMEM((tm, d), BF16), pltpu.VMEM((tm, d), F32)],
        compiler_params=_params(("parallel",)),
        name="ffn",
    )(x2, gain.reshape(1, d), wg, wu, wo)


def _proj_kernel(x_ref, g_ref, w_ref, o_ref, h_ref):
    @pl.when(pl.program_id(1) == 0)
    def _():
        h_ref[...] = _rms(x_ref[...], g_ref[...]).astype(BF16)

    o_ref[...] = jnp.dot(h_ref[...], w_ref[...], preferred_element_type=F32)


def _proj_t_kernel(x_ref, g_ref, w_ref, wt_ref, o_ref, ot_ref, *, tn):
    h = _rms(x_ref[...], g_ref[...]).astype(BF16)
    ot_ref[...] = _dot_nt(wt_ref[...], h)
    for c in range(0, w_ref.shape[1], tn):
        o_ref[:, c:c + tn] = jnp.dot(h, w_ref[:, c:c + tn], preferred_element_type=F32)


def _norm_proj_wide(x2, gain, w, wt, *, tm, tn):
    n, d = x2.shape
    nc = w.shape[1]
    rows = wt.shape[0]
    return pl.pallas_call(
        functools.partial(_proj_t_kernel, tn=tn),
        grid=(n // tm,),
        in_specs=[pl.BlockSpec((tm, d), lambda i: (i, 0)), _resident((1, d)),
                  _resident((d, nc)), _resident((rows, d))],
        out_specs=[pl.BlockSpec((tm, nc), lambda i: (i, 0)),
                   pl.BlockSpec((rows, tm), lambda i: (0, i))],
        out_shape=[jax.ShapeDtypeStruct((n, nc), F32), jax.ShapeDtypeStruct((rows, n), F32)],
        compiler_params=_params(("parallel",)),
        name="mix_proj",
    )(x2, gain.reshape(1, d), w.astype(BF16), wt.astype(BF16))


def _norm_proj(x2, gain, w, *, tm, tn):
    n, d = x2.shape
    nc = w.shape[1]
    in_specs = [pl.BlockSpec((tm, d), lambda i, j: (i, 0)),
                pl.BlockSpec((1, d), lambda i, j: (0, 0)),
                pl.BlockSpec((d, tn), lambda i, j: (0, j))]
    out_specs = pl.BlockSpec((tm, tn), lambda i, j: (i, j))
    out_shape = jax.ShapeDtypeStruct((n, nc), F32)
    return pl.pallas_call(
        _proj_kernel,
        grid=(n // tm, nc // tn),
        in_specs=in_specs, out_specs=out_specs, out_shape=out_shape,
        scratch_shapes=[pltpu.VMEM((tm, d), BF16)],
        compiler_params=_params(("parallel", "arbitrary")),
        name="norm_proj",
    )(x2, gain.reshape(1, d), w.astype(BF16))


def _outproj_kernel(*refs, n_in):
    x_ref = refs[0]
    a_refs = refs[1:1 + n_in]
    w_refs = refs[1 + n_in:1 + 2 * n_in]
    o_ref = refs[1 + 2 * n_in]
    acc = x_ref[...]
    for a_ref, w_ref in zip(a_refs, w_refs):
        acc = acc + _dot(a_ref[...], w_ref[...])
    o_ref[...] = acc


def _out_proj(x2, acts, ws, *, tm):
    n, d = x2.shape
    n_in = len(acts)
    in_specs = [pl.BlockSpec((tm, d), lambda i: (i, 0))]
    in_specs += [pl.BlockSpec((tm, a.shape[1]), lambda i: (i, 0)) for a in acts]
    in_specs += [_resident(w.shape) for w in ws]
    return pl.pallas_call(
        functools.partial(_outproj_kernel, n_in=n_in),
        grid=(n // tm,),
        in_specs=in_specs,
        out_specs=pl.BlockSpec((tm, d), lambda i: (i, 0)),
        out_shape=jax.ShapeDtypeStruct((n, d), F32),
        compiler_params=_params(("parallel",)),
        name="out_proj",
    )(x2, *acts, *[w.astype(BF16) for w in ws])


def _fox_c_kernel(f_ref, b_ref, c_ref, *, seq):
    z = f_ref[...] + b_ref[...]
    logf = jnp.minimum(z, 0.0) - jnp.log1p(jnp.exp(-jnp.abs(z)))
    r = lax.broadcasted_iota(jnp.int32, (LANES, LANES), 0)
    c = lax.broadcasted_iota(jnp.int32, (LANES, LANES), 1)
    upper = (r <= c).astype(F32)
    carry = jnp.zeros((f_ref.shape[0], 1), F32)
    for ch in range(seq // LANES):
        seg = logf[:, ch * LANES:(ch + 1) * LANES]
        cs = jnp.dot(seg, upper, precision=HIGHEST, preferred_element_type=F32) + carry
        c_ref[:, ch * LANES:(ch + 1) * LANES] = cs
        carry = cs[:, LANES - 1:LANES]


def _fox_c(f_t, bias8, *, batch, seq):
    rows = f_t.shape[0]
    return pl.pallas_call(
        functools.partial(_fox_c_kernel, seq=seq),
        grid=(batch,),
        in_specs=[pl.BlockSpec((rows, seq), lambda b: (0, b)),
                  pl.BlockSpec((rows, 1), lambda b: (0, 0))],
        out_specs=pl.BlockSpec((None, rows, seq), lambda b: (b, 0, 0)),
        out_shape=jax.ShapeDtypeStruct((batch, rows, seq), F32),
        compiler_params=_params(("parallel",)),
        name="fox_c",
    )(f_t, bias8)


def _softmax_block(s, m, acc, v):
    m_new = jnp.maximum(m, jnp.max(s, axis=-1, keepdims=True))
    alpha = jnp.exp2(m - m_new)
    return m_new, alpha * acc + _dot(jnp.exp2(s - m_new), v)


def _softmax_first(s, v):
    m = jnp.max(s, axis=-1, keepdims=True)
    return m, _dot(jnp.exp2(s - m), v)


def _softmax_finish(acc0, acc1, lo):
    return jnp.where(lo, acc0 / pltpu.roll(acc0, HEAD_DIM, axis=1),
                     acc1 / pltpu.roll(acc1, HEAD_DIM, axis=1))


def _store_values(vb_ref, rows, v, lo):
    vb_ref[0, rows, :] = jnp.where(lo, v, 1.0).astype(BF16)
    vb_ref[1, rows, :] = jnp.where(lo, 1.0, v).astype(BF16)


def _fox_kernel(q_ref, k_ref, v_ref, c_ref, qg_ref, kg_ref, o_ref, kn_ref, vb_ref, *, tq, seq, tks):
    i = pl.program_id(2)
    lo = _lo_mask()

    @pl.when(i == 0)
    def _():
        def prep(j, carry):
            rows = pl.ds(pl.multiple_of(j * tq, tq), tq)
            kn_ref[rows, :] = _pair_rms(k_ref[rows, :], kg_ref[...], lo).astype(BF16)
            _store_values(vb_ref, rows, v_ref[rows, :], lo)
            return carry
        lax.fori_loop(0, seq // tq, prep, 0)

    q = _pair_rms(q_ref[...], qg_ref[...], lo)
    qh = (jnp.where(lo, q, 0.0).astype(BF16), jnp.where(lo, 0.0, q).astype(BF16))

    start = pl.multiple_of(i * tq, tq)
    c_first = c_ref[:, pl.ds(start, tq)][:, 0:1]
    n_full = (i * tq) // tks
    rem = pl.multiple_of(n_full * tks, tks)

    k_r = kn_ref[pl.ds(rem, tks), :]
    q_pos = start + lax.broadcasted_iota(jnp.int32, (tq, 1), 0)
    k_pos = rem + lax.broadcasted_iota(jnp.int32, (1, tks), 1)
    visible = k_pos <= q_pos
    bias_r = (c_first - c_ref[:, pl.ds(rem, tks)]) * LOG2E
    state = []
    for h in range(2):
        s = _dot_nt(qh[h], k_r) + bias_r[h:h + 1, :]
        state.extend(_softmax_first(jnp.where(visible, s, MASKED), vb_ref[h, pl.ds(rem, tks), :]))

    def body(j, st):
        off = pl.multiple_of(j * tks, tks)
        kj = kn_ref[pl.ds(off, tks), :]
        bias = (c_first - c_ref[:, pl.ds(off, tks)]) * LOG2E
        out = []
        for h in range(2):
            s = _dot_nt(qh[h], kj) + bias[h:h + 1, :]
            out.extend(_softmax_block(s, st[2 * h], st[2 * h + 1], vb_ref[h, pl.ds(off, tks), :]))
        return tuple(out)

    _, a0, _, a1 = lax.fori_loop(0, n_full, body, tuple(state))
    o_ref[...] = _softmax_finish(a0, a1, lo)


def _fox_attention(z, c4, q_gain, k_gain, *, tq):
    batch, seq, _ = z.shape
    n_pairs = c4.shape[1]
    scale = HEAD_DIM ** -0.5 * LOG2E
    qg = jnp.tile(q_gain, 2).reshape(1, PAIR) * scale
    kg = jnp.tile(k_gain, 2).reshape(1, PAIR)
    tks = min(KEY_SPAN, seq)
    assert seq % tks == 0 and tks % tq == 0
    return pl.pallas_call(
        functools.partial(_fox_kernel, tq=tq, seq=seq, tks=tks),
        grid=(batch, n_pairs, seq // tq),
        in_specs=[pl.BlockSpec((None, tq, PAIR), lambda b, p, i: (b, i, p)),
                  pl.BlockSpec((None, seq, PAIR), lambda b, p, i: (b, 0, n_pairs + p)),
                  pl.BlockSpec((None, seq, PAIR), lambda b, p, i: (b, 0, 2 * n_pairs + p)),
                  pl.BlockSpec((None, None, 2, seq), lambda b, p, i: (b, p, 0, 0)),
                  pl.BlockSpec((1, PAIR), lambda b, p, i: (0, 0)),
                  pl.BlockSpec((1, PAIR), lambda b, p, i: (0, 0))],
        out_specs=pl.BlockSpec((None, tq, PAIR), lambda b, p, i: (b, i, p)),
        out_shape=jax.ShapeDtypeStruct((batch, seq, n_pairs * PAIR), F32),
        scratch_shapes=[pltpu.VMEM((seq, PAIR), BF16), pltpu.VMEM((2, seq, PAIR), BF16)],
        compiler_params=_params(("parallel", "parallel", "arbitrary")),
        name="fox_attn",
    )(z, z, z, c4, qg, kg)


def _rope(x, cos_f, sin_f):
    lane = lax.broadcasted_iota(jnp.int32, (1, PAIR), 1) % HEAD_DIM
    partner = jnp.where(lane < ROPE_HALF,
                        pltpu.roll(x, PAIR - ROPE_HALF, axis=1),
                        pltpu.roll(x, ROPE_HALF, axis=1))
    return x * cos_f + partner * sin_f


def _topk_mask(gate_t, n_valid):
    blk = lax.broadcasted_iota(jnp.int32, gate_t.shape, 0)
    blk_f = blk.astype(F32)
    neg = -jnp.inf
    g = jnp.where(blk < n_valid, gate_t, neg)
    sel = jnp.zeros(gate_t.shape, F32)
    for _ in range(MOBA_TOPK):
        mx = jnp.max(g, axis=0, keepdims=True)
        first = jnp.min(jnp.where(g == mx, blk_f, float(LANES)), axis=0, keepdims=True)
        pick = jnp.logical_and(blk_f == first, mx > neg)
        sel = jnp.where(pick, 1.0, sel)
        g = jnp.where(pick, neg, g)
    return sel


def _moba_kernel(q_ref, k_ref, v_ref, cq_ref, sq_ref, ck_ref, sk_ref, qg_ref, kg_ref,
                 o_ref, kn_ref, vb_ref, km_ref, *, tq, tb, seq, tks):
    i = pl.program_id(2)
    lo = _lo_mask()
    n_blk = seq // tb

    @pl.when(i == 0)
    def _():
        km_ref[...] = jnp.zeros_like(km_ref)
        for n in range(n_blk):
            rows = pl.ds(n * tb, tb)
            kn = _rope(_pair_rms(k_ref[rows, :], kg_ref[...], lo), ck_ref[rows, :], sk_ref[rows, :])
            kn_ref[rows, :] = kn.astype(BF16)
            _store_values(vb_ref, rows, v_ref[rows, :], lo)
            km_ref[n:n + 1, :] = jnp.mean(kn, axis=0, keepdims=True)

    q = _rope(_pair_rms(q_ref[...], qg_ref[...], lo), cq_ref[...], sq_ref[...])
    q_lo = jnp.where(lo, q, 0.0)
    q_hi = jnp.where(lo, 0.0, q)
    km = km_ref[...]
    scale = HEAD_DIM ** -0.5 * LOG2E
    rr = lax.broadcasted_iota(jnp.int32, (tq, 1), 0)
    sub = jnp.zeros((tq, 1), jnp.int32)
    for edge in range(tb, tq, tb):
        sub = sub + (rr >= edge).astype(jnp.int32)
    q_blk = i * (tq // tb) + sub
    row_in_blk = rr - sub * tb

    col = lax.broadcasted_iota(jnp.int32, (1, tq), 1)
    q_blk_row = jnp.full((1, tq), i * (tq // tb), jnp.int32)
    for edge in range(tb, tq, tb):
        q_blk_row = q_blk_row + (col >= edge).astype(jnp.int32)
    nb8 = -(-n_blk // 8) * 8
    to_lanes = (lax.broadcasted_iota(jnp.int32, (nb8, LANES), 0)
                == lax.broadcasted_iota(jnp.int32, (nb8, LANES), 1)).astype(F32)
    qh, sel = [], []
    for qm in (q_lo, q_hi):
        gate_t = lax.dot_general(km[:nb8], qm, (((1,), (1,)), ((), ())), precision=HIGHEST,
                                 preferred_element_type=F32)
        sel.append(_dot_tn(_topk_mask(gate_t, q_blk_row), to_lanes))
        qh.append((qm * scale).astype(BF16))

    per_span = tks // tb
    n_full = (i * tq) // tks
    lane = lax.broadcasted_iota(jnp.int32, (1, LANES), 1)
    cc = lax.broadcasted_iota(jnp.int32, (1, tb), 1)

    def masked_scores(h, span, k_span, with_own):
        s = _dot_nt(qh[h], k_span)
        parts = []
        for b in range(per_span):
            blk = span * per_span + b
            chosen = jnp.max(jnp.where(lane == blk, sel[h], 0.0), axis=-1, keepdims=True)
            keep = chosen > 0.0
            if with_own:
                last_visible = jnp.where(q_blk == blk, row_in_blk, -1)
                keep = jnp.logical_or(keep, cc <= last_visible)
            parts.append(jnp.where(keep, s[:, b * tb:(b + 1) * tb], MASKED))
        return jnp.concatenate(parts, axis=1)

    rem = pl.multiple_of(n_full * tks, tks)
    k_r = kn_ref[pl.ds(rem, tks), :]
    state = []
    for h in range(2):
        state.extend(_softmax_first(masked_scores(h, n_full, k_r, True),
                                    vb_ref[h, pl.ds(rem, tks), :]))

    def body(j, st):
        off = pl.multiple_of(j * tks, tks)
        kj = kn_ref[pl.ds(off, tks), :]
        out = []
        for h in range(2):
            s = masked_scores(h, j, kj, False)
            out.extend(_softmax_block(s, st[2 * h], st[2 * h + 1], vb_ref[h, pl.ds(off, tks), :]))
        return tuple(out)

    _, a0, _, a1 = lax.fori_loop(0, n_full, body, tuple(state))
    o_ref[...] = _softmax_finish(a0, a1, lo)


def _moba_attention(z, cos_f, sin_f, q_gain, k_gain, *, col0, n_pairs):
    batch, seq, _ = z.shape
    tb = MOBA_BLOCK
    tq = 2 * tb if seq % (2 * tb) == 0 else tb
    assert seq % tq == 0 and seq // tb <= LANES
    qg = jnp.tile(q_gain, 2).reshape(1, PAIR)
    kg = jnp.tile(k_gain, 2).reshape(1, PAIR)
    tile = pl.BlockSpec((None, tq, PAIR), lambda b, p, i: (b, i, 0))
    full = pl.BlockSpec((None, seq, PAIR), lambda b, p, i: (b, 0, 0))
    tks = min(KEY_SPAN, seq)
    assert seq % tks == 0 and tks % tq == 0
    return pl.pallas_call(
        functools.partial(_moba_kernel, tq=tq, tb=tb, seq=seq, tks=tks),
        grid=(batch, n_pairs, seq // tq),
        in_specs=[pl.BlockSpec((None, tq, PAIR), lambda b, p, i: (b, i, col0 + p)),
                  pl.BlockSpec((None, seq, PAIR), lambda b, p, i: (b, 0, col0 + n_pairs + p)),
                  pl.BlockSpec((None, seq, PAIR), lambda b, p, i: (b, 0, col0 + 2 * n_pairs + p)),
                  tile, tile, full, full,
                  pl.BlockSpec((1, PAIR), lambda b, p, i: (0, 0)),
                  pl.BlockSpec((1, PAIR), lambda b, p, i: (0, 0))],
        out_specs=pl.BlockSpec((None, tq, PAIR), lambda b, p, i: (b, i, p)),
        out_shape=jax.ShapeDtypeStruct((batch, seq, n_pairs * PAIR), F32),
        scratch_shapes=[pltpu.VMEM((seq, PAIR), BF16), pltpu.VMEM((2, seq, PAIR), BF16),
                        pltpu.VMEM((LANES, PAIR), F32)],
        compiler_params=_params(("parallel", "parallel", "arbitrary")),
        name="moba_attn",
    )(z, z, z, cos_f, sin_f, cos_f, sin_f, qg, kg)


def _tri_inverses(n_mats, t):
    r = lax.broadcasted_iota(jnp.int32, (t, t), 0)
    c = lax.broadcasted_iota(jnp.int32, (t, t), 1)
    eye = (r == c).astype(F32)
    ms = [eye + n for n in n_mats]
    ps = list(n_mats)
    for _ in range(int(math.log2(t)) - 1):
        ps = [_dot(p, p) for p in ps]
        ms = [m + _dot(m, p) for m, p in zip(ms, ps)]
    return ms


def _headsum(x, lo):
    return jnp.concatenate([_halfsum(x[:, c:c + PAIR], lo) for c in range(0, x.shape[1], PAIR)],
                           axis=1)


def _rwkv_tile(r, k2, v, kkn, a, lw, st_ref, lo, t):
    tc, width = r.shape
    n_ch, n_pairs = tc // t, width // PAIR
    rr = lax.broadcasted_iota(jnp.int32, (t, t), 0)
    cc = lax.broadcasted_iota(jnp.int32, (t, t), 1)
    tril = (cc <= rr).astype(F32)
    r2 = lax.broadcasted_iota(jnp.int32, (t, 2 * t), 0)
    c2 = lax.broadcasted_iota(jnp.int32, (t, 2 * t), 1)
    strict_k = jnp.logical_and(c2 >= t, c2 - t < r2)
    incl2 = jnp.where(c2 >= t, c2 - t, c2) <= r2
    same_head = (lax.broadcasted_iota(jnp.int32, (PAIR, PAIR), 0) < HEAD_DIM) == (
        lax.broadcasted_iota(jnp.int32, (PAIR, PAIR), 1) < HEAD_DIM)
    beta = kkn * a
    units = [(ch, p) for ch in range(n_ch) for p in range(n_pairs)]

    left, right, right_end, decay_end, vs = [], [], [], [], []
    for ch in range(n_ch):
        rows = slice(ch * t, (ch + 1) * t)
        cs = jnp.dot(tril, lw[rows], precision=HIGHEST, preferred_element_type=F32)
        c_last = cs[t - 1:t, :]
        e_inc, e_exc, e_inv, e_rem = (jnp.exp(cs), jnp.exp(cs - lw[rows]), jnp.exp(-cs),
                                      jnp.exp(c_last - cs))
        left.append(jnp.concatenate([-kkn[rows] * e_exc, r[rows] * e_inc], axis=0))
        right.append(jnp.concatenate([beta[rows] * e_inv, k2[rows] * e_inv], axis=0))
        right_end.append(jnp.concatenate([beta[rows] * e_rem, k2[rows] * e_rem], axis=0))
        decay_end.append(jnp.exp(c_last))
        vs.append(v[rows])

    def pc(x, p):
        return x[:, p * PAIR:(p + 1) * PAIR]

    grams = {}
    for ch, p in units:
        lf, rt = pc(left[ch], p), pc(right[ch], p)
        grams[ch, p, 0] = _dot_nt(jnp.where(lo, lf, 0.0), rt)
        grams[ch, p, 1] = _dot_nt(jnp.where(lo, 0.0, lf), rt)
    keys = [(ch, p, h) for ch, p in units for h in range(2)]
    invs = dict(zip(keys, _tri_inverses(
        [jnp.where(cc < rr, grams[k][:t, :t], 0.0) for k in keys], t)))

    x_keys, lower = {}, {}
    for ch, p in units:
        vv = jnp.concatenate([pc(vs[ch], p)] * 2, axis=0)
        x_keys[ch, p] = jnp.where(lo, _dot(jnp.where(strict_k, grams[ch, p, 0][:t], 0.0), vv),
                                  _dot(jnp.where(strict_k, grams[ch, p, 1][:t], 0.0), vv))
        for h in range(2):
            lower[ch, p, h] = jnp.where(incl2, grams[ch, p, h][t:], 0.0).astype(BF16)

    state = [st_ref[p] for p in range(n_pairs)]
    ys = []
    for ch in range(n_ch):
        y_ch = []
        for p in range(n_pairs):
            from_state = _dot_nt(pc(left[ch], p), state[p])
            x = from_state[:t] + x_keys[ch, p]
            u = jnp.where(lo, _dot(invs[ch, p, 0], x), _dot(invs[ch, p, 1], x))
            uv = jnp.concatenate([u, pc(vs[ch], p)], axis=0)
            y_ch.append(from_state[t:] + jnp.where(lo, _dot(lower[ch, p, 0], uv),
                                                   _dot(lower[ch, p, 1], uv)))
            state[p] = jnp.where(same_head, state[p] * pc(decay_end[ch], p)
                                 + _dot_tn(uv, pc(right_end[ch], p)), 0.0)
        ys.append(jnp.concatenate(y_ch, axis=1))
    for p in range(n_pairs):
        st_ref[p] = state[p]
    return jnp.concatenate(ys, axis=0)


def _rwkv_kernel(r_ref, k_ref, v_ref, l_ref, mur_ref, muk_ref, muv_ref, mul_ref,
                 w0_ref, w2_ref, a0_ref, a2_ref, g2_ref, kk_ref, ka_ref, rk_ref,
                 lnw_ref, lnb_ref, o_ref, st_ref, pr_ref, pk_ref, pv_ref, pl_ref, *, tc, t):
    @pl.when(pl.program_id(1) == 0)
    def _():
        st_ref[...] = jnp.zeros_like(st_ref)
        pr_ref[...] = jnp.zeros_like(pr_ref)
        pk_ref[...] = jnp.zeros_like(pk_ref)
        pv_ref[...] = jnp.zeros_like(pv_ref)
        pl_ref[...] = jnp.zeros_like(pl_ref)

    row = lax.broadcasted_iota(jnp.int32, (tc, 1), 0)

    def shifted(x_ref, p_ref, mu_ref):
        x = x_ref[...]
        prev = jnp.where(row == 0, p_ref[...], pltpu.roll(x, 1, axis=0))
        p_ref[...] = x[tc - 1:tc, :]
        return x + (prev - x) * mu_ref[...]

    r = shifted(r_ref, pr_ref, mur_ref)
    k = shifted(k_ref, pk_ref, muk_ref)
    v = shifted(v_ref, pv_ref, muv_ref)
    lora = shifted(l_ref, pl_ref, mul_ref)
    wa = lora[:, :LANES]
    gl = lora[:, LANES:]
    w_raw = w0_ref[...] + _dot(jnp.tanh(wa), w2_ref[...])
    lw = -math.exp(-0.5) * jax.nn.sigmoid(w_raw)
    a = jax.nn.sigmoid(a0_ref[...] + _dot(wa, a2_ref[...]))
    g = _dot(jax.nn.sigmoid(gl), g2_ref[...])
    kk = k * kk_ref[...]
    k2 = k * (1.0 + (a - 1.0) * ka_ref[...])

    lo = _lo_mask()
    kkn = kk / jnp.maximum(jnp.sqrt(_headsum(kk * kk, lo)), 1e-12)
    bonus = _headsum(r * k2 * rk_ref[...], lo) * v
    y = _rwkv_tile(r, k2, v, kkn, a, lw, st_ref, lo, t)
    mean = _headsum(y, lo) * (1.0 / HEAD_DIM)
    d = y - mean
    var = _headsum(d * d, lo) * (1.0 / HEAD_DIM)
    yn = d * lax.rsqrt(var + RWKV_GN_EPS) * lnw_ref[...] + lnb_ref[...]
    o_ref[...] = (yn + bonus) * g


def _rwkv(z, prm, *, col_r, col_l, width, lora_w, tc):
    batch, seq, _ = z.shape
    n_pairs = width // PAIR
    blk = lambda c: pl.BlockSpec((None, tc, width), lambda b, i: (b, i, c))
    vec = lambda w: pl.BlockSpec((1, w), lambda b, i: (0, 0))
    mat = lambda s: pl.BlockSpec(s, lambda b, i: (0, 0))
    in_specs = [blk(col_r), blk(col_r + 1), blk(col_r + 2),
                pl.BlockSpec((None, tc, lora_w), lambda b, i: (b, i, col_l)),
                vec(width), vec(width), vec(width), vec(lora_w),
                vec(width), mat((LANES, width)), vec(width), mat((LANES, width)),
                mat((lora_w - LANES, width)),
                vec(width), vec(width), vec(width), vec(width), vec(width)]
    return pl.pallas_call(
        functools.partial(_rwkv_kernel, tc=tc, t=RWKV_CHUNK),
        grid=(batch, seq // tc),
        in_specs=in_specs,
        out_specs=pl.BlockSpec((None, tc, width), lambda b, i: (b, i, 0)),
        out_shape=jax.ShapeDtypeStruct((batch, seq, width), F32),
        scratch_shapes=[pltpu.VMEM((n_pairs, PAIR, PAIR), F32),
                        pltpu.VMEM((1, width), F32), pltpu.VMEM((1, width), F32),
                        pltpu.VMEM((1, width), F32), pltpu.VMEM((1, lora_w), F32)],
        compiler_params=_params(("parallel", "arbitrary")),
        name="rwkv7",
    )(z, z, z, z, *prm)


def _xattn_kernel(x_ref, kv_ref, g_ref, wq_ref, wo_ref, qg_ref, kg_ref, o_ref, kn_ref, vb_ref):
    width = XATTN_HEADS * XATTN_HEAD_DIM

    @pl.when(pl.program_id(1) == 0)
    def _():
        for h in range(XATTN_HEADS):
            cols = slice(h * XATTN_HEAD_DIM, (h + 1) * XATTN_HEAD_DIM)
            kn_ref[:, cols] = _rms(kv_ref[:, cols], kg_ref[...]).astype(BF16)
        vb_ref[...] = kv_ref[:, width:].astype(BF16)

    x = x_ref[...]
    q = _dot(_rms(x, g_ref[...]), wq_ref[...])
    scale = XATTN_HEAD_DIM ** -0.5
    outs = []
    for h in range(XATTN_HEADS):
        cols = slice(h * XATTN_HEAD_DIM, (h + 1) * XATTN_HEAD_DIM)
        qh = _rms(q[:, cols], qg_ref[...])
        s = _dot_nt(qh, kn_ref[:, cols]) * scale
        m = jnp.max(s, axis=-1, keepdims=True)
        p = jnp.exp(s - m)
        l = jnp.sum(p, axis=-1, keepdims=True)
        outs.append(_dot(p / l, vb_ref[:, cols]))
    o_ref[...] = x + _dot(jnp.concatenate(outs, axis=-1), wo_ref[...])


def _xattn(x, kv, gain, w_q, w_o, q_gain, k_gain, *, tq):
    batch, seq, d = x.shape
    mem_len = kv.shape[1]
    width = XATTN_HEADS * XATTN_HEAD_DIM
    return pl.pallas_call(
        _xattn_kernel,
        grid=(batch, seq // tq),
        in_specs=[pl.BlockSpec((None, tq, d), lambda b, i: (b, i, 0)),
                  pl.BlockSpec((None, mem_len, 2 * width), lambda b, i: (b, 0, 0)),
                  _resident((1, d)), _resident((d, width)), _resident((width, d)),
                  _resident((1, XATTN_HEAD_DIM)), _resident((1, XATTN_HEAD_DIM))],
        out_specs=pl.BlockSpec((None, tq, d), lambda b, i: (b, i, 0)),
        out_shape=jax.ShapeDtypeStruct((batch, seq, d), F32),
        scratch_shapes=[pltpu.VMEM((mem_len, width), BF16), pltpu.VMEM((mem_len, width), BF16)],
        compiler_params=_params(("parallel", "arbitrary")),
        name="xattn",
    )(x, kv, gain.reshape(1, d), w_q.astype(BF16), w_o.astype(BF16),
      q_gain.reshape(1, -1), k_gain.reshape(1, -1))


def _rope_tables(positions):
    inv_freq = ROPE_THETA ** (-jnp.arange(0, ROPE_DIM, 2, dtype=F32) / ROPE_DIM)
    ang = positions.astype(F32)[..., None] * inv_freq
    cos, sin = jnp.cos(ang), jnp.sin(ang)
    rest = HEAD_DIM - ROPE_DIM
    ones = jnp.ones(cos.shape[:-1] + (rest,), F32)
    zeros = jnp.zeros(cos.shape[:-1] + (rest,), F32)
    cos_f = jnp.concatenate([cos, cos, ones], axis=-1)
    sin_f = jnp.concatenate([-sin, sin, zeros], axis=-1)
    return jnp.tile(cos_f, 2), jnp.tile(sin_f, 2)


def _pad_rows(w, rows, at):
    out = jnp.zeros((rows, w.shape[1]), w.dtype)
    return lax.dynamic_update_slice(out, w, (at, 0))


def _mixers(x, cos_f, sin_f, mix_norm, mix_w_in, fox_f_bias, fox_q_gain, fox_k_gain,
            moba_q_gain, moba_k_gain, rwkv_mu, rwkv_w0, rwkv_w2, rwkv_a0, rwkv_a2, rwkv_g2,
            rwkv_k_k, rwkv_k_a, rwkv_r_k, rwkv_ln_w, rwkv_ln_b, *, tm):
    batch, seq, d = x.shape
    n = batch * seq
    fox_heads = fox_f_bias.shape[0]
    fox_w = fox_heads * HEAD_DIM
    moba_w = fox_w
    rwkv_w = rwkv_w0.shape[0]
    dec_l, icl_l, gate_l = rwkv_w2.shape[0], rwkv_a2.shape[0], rwkv_g2.shape[0]
    assert fox_w % PAIR == 0 and rwkv_w % PAIR == 0 and dec_l + icl_l == LANES
    fox_in = 3 * fox_w + fox_heads
    moba_in = 3 * moba_w
    lora_in = dec_l + icl_l + gate_l
    lora_w = LANES * (-(-(lora_in + fox_heads) // LANES))
    in_width = mix_w_in.shape[1]
    head = 3 * fox_w + moba_in + 3 * rwkv_w
    assert head % lora_w == 0 and head % rwkv_w == 0
    r0 = fox_in + moba_in
    w_perm = jnp.concatenate([
        mix_w_in[:, :3 * fox_w], mix_w_in[:, fox_in:fox_in + moba_in],
        mix_w_in[:, r0:r0 + 3 * rwkv_w], mix_w_in[:, r0 + 3 * rwkv_w:in_width],
        mix_w_in[:, 3 * fox_w:fox_in],
        jnp.zeros((d, lora_w - lora_in - fox_heads), F32)], axis=1)
    total = w_perm.shape[1]
    wf_t = _pad_rows(mix_w_in[:, 3 * fox_w:fox_in].T, 8, 0)

    x2 = x.reshape(n, d)
    tn = total // 9 if total % (9 * LANES) == 0 else LANES
    z2, f_t = _norm_proj_wide(x2, mix_norm, w_perm, wf_t, tm=tm, tn=tn)
    z = z2.reshape(batch, seq, total)

    bias8 = _pad_rows(fox_f_bias.reshape(fox_heads, 1), 8, 0)
    c = _fox_c(f_t, bias8, batch=batch, seq=seq)
    c4 = c[:, :fox_heads].reshape(batch, fox_heads // 2, 2, seq)
    y_fox = _fox_attention(z, c4, fox_q_gain, fox_k_gain, tq=512 if seq % 512 == 0 else 256)
    y_moba = _moba_attention(z, cos_f, sin_f, moba_q_gain, moba_k_gain,
                             col0=3 * fox_w // PAIR, n_pairs=moba_w // PAIR)

    mu_l = jnp.concatenate([rwkv_mu[3 * rwkv_w:], jnp.zeros((lora_w - lora_in,), F32)])
    prm = [rwkv_mu[:rwkv_w], rwkv_mu[rwkv_w:2 * rwkv_w], rwkv_mu[2 * rwkv_w:3 * rwkv_w], mu_l,
           rwkv_w0, _pad_rows(rwkv_w2, LANES, 0).astype(BF16),
           rwkv_a0, _pad_rows(rwkv_a2, LANES, dec_l).astype(BF16),
           _pad_rows(rwkv_g2, lora_w - LANES, 0).astype(BF16),
           rwkv_k_k, rwkv_k_a, rwkv_r_k.reshape(-1), rwkv_ln_w, rwkv_ln_b]
    prm = [p.reshape(1, -1) if p.ndim == 1 else p for p in prm]
    y_rwkv = _rwkv(z, prm, col_r=(3 * fox_w + moba_in) // rwkv_w, col_l=head // lora_w,
                   width=rwkv_w, lora_w=lora_w, tc=256)

    return y_fox, y_rwkv, y_moba


def _mixer_layer(x, cos_f, sin_f, mix_norm, mix_w_in, mix_w_out, *mixer_params, tm):
    batch, seq, d = x.shape
    n = batch * seq
    ys = _mixers(x, cos_f, sin_f, mix_norm, mix_w_in, *mixer_params, tm=tm)
    acts = [y.reshape(n, y.shape[-1]) for y in ys]
    bounds = [0]
    for a in acts:
        bounds.append(bounds[-1] + a.shape[1])
    ws = [mix_w_out[lo:hi] for lo, hi in zip(bounds[:-1], bounds[1:])]
    return _out_proj(x.reshape(n, d), acts, ws, tm=tm).reshape(batch, seq, d)


def kernel(x, mem, positions, ffn1_norm, ffn1_w_in, ffn1_w_out, mix_norm, mix_w_in, mix_w_out, fox_f_bias, fox_q_gain, fox_k_gain, moba_q_gain, moba_k_gain, rwkv_mu, rwkv_w0, rwkv_w2, rwkv_a0, rwkv_a2, rwkv_g2, rwkv_k_k, rwkv_k_a, rwkv_r_k, rwkv_ln_w, rwkv_ln_b, xattn_norm, xattn_mem_norm, xattn_w_q, xattn_w_kv, xattn_q_gain, xattn_k_gain, xattn_w_out, ffn2_norm, ffn2_w_in, ffn2_w_out):
    batch, seq, d = x.shape
    n = batch * seq
    depth = ffn1_norm.shape[0]
    tm = 512 if n % 512 == 0 else 256
    tf = 2816
    cos_f, sin_f = _rope_tables(positions)
    mem2 = mem.reshape(-1, d)
    for l in range(depth):
        x = _ffn(x.reshape(n, d), ffn1_norm[l], ffn1_w_in[l], ffn1_w_out[l], tm=tm, tf=tf)
        x = x.reshape(batch, seq, d)
        x = _mixer_layer(x, cos_f, sin_f, mix_norm[l], mix_w_in[l], mix_w_out[l], fox_f_bias[l],
                         fox_q_gain[l], fox_k_gain[l], moba_q_gain[l], moba_k_gain[l], rwkv_mu[l],
                         rwkv_w0[l], rwkv_w2[l], rwkv_a0[l], rwkv_a2[l], rwkv_g2[l], rwkv_k_k[l],
                         rwkv_k_a[l], rwkv_r_k[l], rwkv_ln_w[l], rwkv_ln_b[l], tm=tm)
        kv = _norm_proj(mem2, xattn_mem_norm[l], xattn_w_kv[l], tm=256, tn=512)
        x = _xattn(x, kv.reshape(batch, -1, kv.shape[-1]), xattn_norm[l], xattn_w_q[l],
                   xattn_w_out[l], xattn_q_gain[l], xattn_k_gain[l], tq=512 if seq % 512 == 0 else 256)
        x = _ffn(x.reshape(n, d), ffn2_norm[l], ffn2_w_in[l], ffn2_w_out[l], tm=tm, tf=tf)
        x = x.reshape(batch, seq, d)
    return x
```
